```python
import math
import jax, jax.numpy as jnp
from jax import lax
import numpy as np

D_MODEL = 2048
BATCH = 2
SEQ = 4096
DEPTH = 4
DEC_BATCH = 8
DEC_SEQ = 1
PAST_LEN = 16384
PAGE_SIZE = 128

N_MIXERS = 4
N_A = (DEPTH + 3) // 4
N_B = (DEPTH + 2) // 4
N_C = (DEPTH + 1) // 4
N_D = DEPTH // 4
EPS = 1e-6
D_FF = 5632
N_MEM = 256
MEM_HEADS = 4
MEM_HD = 128
MEM_W = MEM_HEADS * MEM_HD
MEM_SCALE = MEM_HD ** -0.5
GLA_HEADS = 4
GLA_DK = D_MODEL // 2 // GLA_HEADS
GLA_DV = D_MODEL // GLA_HEADS
GLA_HK = GLA_HEADS * GLA_DK
GLA_HV = GLA_HEADS * GLA_DV
GLA_RANK = 16
GLA_TAU = 16.0
GLA_CHUNK = 64
GLA_IN = 2 * GLA_HK + 2 * GLA_HV + GLA_RANK
SG_CHUNK = 128
SG_WIDTH = D_MODEL
SG_GROUPS = 8
SG_GW = SG_WIDTH // SG_GROUPS
POOL_WINDOWS = (2, 4, 8, 16)
POOL_GW = D_MODEL // len(POOL_WINDOWS)
POOL_BUF = max(POOL_WINDOWS) - 1
SB_HEADS = 16
SB_HD = D_MODEL // SB_HEADS
SB_BLOCK = 128
SB_SCALE = SB_HD ** -0.5
SB_BIAS_INIT = -8.0

kernel_name = 'hybrid_gla_sgmlp_pool_stickbreak_step'


def rms_norm(x, g):
    xf = x.astype(jnp.float32)
    y = xf * lax.rsqrt(jnp.mean(xf * xf, axis=-1, keepdims=True) + EPS)
    return (y * g.astype(jnp.float32)).astype(x.dtype)


def swiglu(h, wi, wo):
    g, u = jnp.split(h @ wi, 2, axis=-1)
    return (jax.nn.silu(g) * u) @ wo


def mem_kv(mem, g_in, w_k, g_k, w_v):
    m = rms_norm(mem, g_in)
    B, N, _ = m.shape
    k = rms_norm((m @ w_k).reshape(B, N, MEM_HEADS, MEM_HD), g_k)
    v = (m @ w_v).reshape(B, N, MEM_HEADS, MEM_HD)
    return k, v


def mem_attend(h, k, v, w_q, g_q, w_o):
    B, T, _ = h.shape
    q = rms_norm((h @ w_q).reshape(B, T, MEM_HEADS, MEM_HD), g_q)
    s = jnp.einsum('bthd,bnhd->bhtn', q, k.astype(q.dtype)).astype(jnp.float32) * MEM_SCALE
    p = jax.nn.softmax(s, axis=-1).astype(h.dtype)
    o = jnp.einsum('bhtn,bnhd->bthd', p, v.astype(h.dtype)).reshape(B, T, MEM_W)
    return o @ w_o


def gla_chunked(q, k, v, log_a, S0):
    B, T, H, _ = q.shape
    c = min(GLA_CHUNK, T)
    n = -(-T // c)
    pad = ((0, 0), (0, n * c - T), (0, 0), (0, 0))

    def chunks(t):
        t = jnp.pad(t, pad)
        return t.reshape(B, n, c, H, t.shape[-1]).transpose(1, 0, 3, 2, 4)

    causal = jnp.tril(jnp.ones((c, c), dtype=bool))

    def step(S, inp):
        qc, kc, vc, lc = inp
        b = jnp.cumsum(lc, axis=2)
        b_last = b[:, :, -1:, :]
        qd = qc * jnp.exp(b)
        att = jnp.einsum('bhtk,bhsk->bhts', qd, kc * jnp.exp(-b))
        att = jnp.where(causal, att, 0.0)
        o = jnp.einsum('bhts,bhsv->bhtv', att, vc) + jnp.einsum('bhtk,bhkv->bhtv', qd, S)
        S = jnp.exp(b_last[:, :, 0, :, None]) * S + jnp.einsum('bhsk,bhsv->bhkv', kc * jnp.exp(b_last - b), vc)
        return S, o

    S, o = lax.scan(step, S0, (chunks(q), chunks(k), chunks(v), chunks(log_a)))
    o = o.transpose(1, 0, 3, 2, 4).reshape(B, n * c, H, o.shape[-1])[:, :T]
    return o, S


def gla_mix(h, S0, w_in, w_gate, b_gate, g_out, w_out):
    B, T, _ = h.shape
    z = h @ w_in
    q, k, v, r, a = jnp.split(z, [GLA_HK, 2 * GLA_HK, 2 * GLA_HK + GLA_HV, 2 * GLA_HK + 2 * GLA_HV], axis=-1)
    log_a = jax.nn.log_sigmoid((a @ w_gate + b_gate).astype(jnp.float32)) / GLA_TAU
    f32 = jnp.float32
    q = q.reshape(B, T, GLA_HEADS, GLA_DK).astype(f32) * (GLA_DK ** -0.5)
    k = k.reshape(B, T, GLA_HEADS, GLA_DK).astype(f32)
    v = v.reshape(B, T, GLA_HEADS, GLA_DV).astype(f32)
    log_a = log_a.reshape(B, T, GLA_HEADS, GLA_DK)
    o, S = gla_chunked(q, k, v, log_a, S0)
    o = rms_norm(o.astype(h.dtype), g_out).reshape(B, T, GLA_HV)
    return (o * jax.nn.silu(r)) @ w_out, S


def sg_mix(h, w_in, g_v, w_s, b_s, w_out):
    B, T, _ = h.shape
    u, v = jnp.split(jax.nn.gelu(h @ w_in), 2, axis=-1)
    v = rms_norm(v, g_v)
    c = min(SG_CHUNK, T)
    n = T // c
    w = jnp.where(jnp.tril(jnp.ones((c, c), dtype=bool)), w_s[:, :c, :c], 0.0)
    vg = v.reshape(B, n, c, SG_GROUPS, SG_GW)
    mixed = jnp.einsum('gts,bnsgd->bntgd', w.astype(v.dtype), vg) + b_s[:, :c].T[:, :, None]
    return (u * mixed.reshape(B, T, SG_WIDTH)) @ w_out, v


def pool_mix(h, prev, w_pool, scale):
    B, T, D = h.shape
    P = prev.shape[1]
    hc = jnp.concatenate([prev.astype(h.dtype), h], axis=1)
    hf = hc.astype(jnp.float32)
    csum = jnp.pad(jnp.cumsum(hf, axis=1), ((0, 0), (1, 0), (0, 0)))
    rows = jnp.arange(P, P + T)
    groups = []
    for g, w in enumerate(POOL_WINDOWS):
        sl = slice(g * POOL_GW, (g + 1) * POOL_GW)
        cg = csum[..., sl]
        lo = jnp.maximum(rows + 1 - w, 0)
        cnt = jnp.minimum(rows + 1, w).astype(jnp.float32)
        mean = (cg[:, P + 1:] - jnp.take(cg, lo, axis=1)) / cnt[:, None]
        groups.append(mean - hf[:, P:, sl])
    pooled = jnp.stack(groups, axis=2).astype(h.dtype)
    y = jnp.einsum('btgc,gcd->btgd', pooled, w_pool).reshape(B, T, D) * scale
    return y, hc[:, P + T - POOL_BUF:]


def sb_project(h, w_qkv, g_q, g_k):
    B, T, _ = h.shape
    qkv = (h @ w_qkv).reshape(B, T, 3, SB_HEADS, SB_HD)
    return rms_norm(qkv[:, :, 0], g_q), rms_norm(qkv[:, :, 1], g_k), qkv[:, :, 2]


def stick_breaking_weights(z, mask):
    log_1m_beta = jnp.where(mask, jax.nn.log_sigmoid(-z), 0.0)
    after = lax.cumsum(log_1m_beta, axis=3, reverse=True)
    between = jnp.concatenate([after[..., 1:], jnp.zeros_like(after[..., :1])], axis=-1)
    return jnp.where(mask, jnp.exp(jax.nn.log_sigmoid(z) + between), 0.0)


def sb_attend(q, k, v, bias, q_offset):
    B, Tq, H, Dh = q.shape
    Tk = k.shape[1]
    c = min(SB_BLOCK, Tq)
    nb = -(-Tq // c)
    qp = jnp.pad(q, ((0, 0), (0, nb * c - Tq), (0, 0), (0, 0)))
    qb = jnp.moveaxis(qp.reshape(B, nb, c, H, Dh), 1, 0)
    kpos = jnp.arange(Tk)
    bias_f = bias.astype(jnp.float32)[None, :, None, None]

    def block(args):
        qi, start = args
        z = jnp.einsum('bqhd,bkhd->bhqk', qi, k).astype(jnp.float32) * SB_SCALE + bias_f
        qpos = q_offset + start + jnp.arange(c)
        mask = kpos[None, :] < qpos[:, None]
        a = stick_breaking_weights(z, mask)
        return jnp.einsum('bhqk,bkhd->bqhd', a.astype(v.dtype), v)

    o = lax.map(block, (qb, jnp.arange(nb) * c))
    return jnp.moveaxis(o, 0, 1).reshape(B, nb * c, H, Dh)[:, :Tq]


def setup_inputs(seed: int = 0) -> dict:
    key = jax.random.key(seed)
    ks = iter(jax.random.split(key, 64))
    f32 = jnp.float32

    def nrm(shape, scale=1.0):
        return jax.random.normal(next(ks), shape, f32) * scale

    def lin(*shape):
        return nrm(shape, shape[-2] ** -0.5)

    def gain(*shape):
        return 1.0 + nrm(shape, 0.05)

    n_pages = PAST_LEN // PAGE_SIZE
    n_used = DEC_BATCH * n_pages
    n_pool = n_used + n_used // 4
    page_table = jax.random.permutation(next(ks), n_pool)[:n_used].reshape(DEC_BATCH, n_pages).astype(jnp.int32)
    D = D_MODEL
    return {
        'x_prompt': nrm((BATCH, SEQ, D)),
        'x_sample': nrm((DEC_BATCH, DEC_SEQ, D)),
        'cache_mem_k': nrm((DEPTH, DEC_BATCH, N_MEM, MEM_HEADS, MEM_HD)),
        'cache_mem_v': nrm((DEPTH, DEC_BATCH, N_MEM, MEM_HEADS, MEM_HD)),
        'state_gla': nrm((N_A, DEC_BATCH, GLA_HEADS, GLA_DK, GLA_DV)),
        'state_pool': nrm((N_C, DEC_BATCH, POOL_BUF, D)),
        'cache_sb_k': nrm((N_D, n_pool, PAGE_SIZE, SB_HEADS, SB_HD)),
        'cache_sb_v': nrm((N_D, n_pool, PAGE_SIZE, SB_HEADS, SB_HD)),
        'page_table': page_table,
        'mem_prompt': nrm((BATCH, N_MEM, D)),
        'ffn_a_norm': gain(DEPTH, D),
        'ffn_a_wi': lin(DEPTH, D, 2 * D_FF),
        'ffn_a_wo': lin(DEPTH, D_FF, D),
        'mix_norm': gain(DEPTH, D),
        'gla_w_in': lin(N_A, D, GLA_IN),
        'gla_w_gate': lin(N_A, GLA_RANK, GLA_HK),
        'gla_b_gate': nrm((N_A, GLA_HK), 0.1),
        'gla_out_norm': gain(N_A, GLA_DV),
        'gla_w_out': lin(N_A, GLA_HV, D),
        'sg_w_in': lin(N_B, D, 2 * SG_WIDTH),
        'sg_v_norm': gain(N_B, SG_WIDTH),
        'sg_w_s': lin(N_B, SG_GROUPS, SG_CHUNK, SG_CHUNK),
        'sg_b_s': gain(N_B, SG_GROUPS, SG_CHUNK),
        'sg_w_out': lin(N_B, SG_WIDTH, D),
        'pool_w': lin(N_C, len(POOL_WINDOWS), POOL_GW, POOL_GW),
        'pool_scale': gain(N_C, D),
        'sb_w_qkv': lin(N_D, D, 3 * D),
        'sb_q_norm': gain(N_D, SB_HD),
        'sb_k_norm': gain(N_D, SB_HD),
        'sb_logit_bias': SB_BIAS_INIT + nrm((N_D, SB_HEADS), 0.1),
        'sb_w_out': lin(N_D, D, D),
        'mem_x_norm': gain(DEPTH, D),
        'mem_in_norm': gain(DEPTH, D),
        'mem_w_q': lin(DEPTH, D, MEM_W),
        'mem_q_norm': gain(DEPTH, MEM_HD),
        'mem_w_k': lin(DEPTH, D, MEM_W),
        'mem_k_norm': gain(DEPTH, MEM_HD),
        'mem_w_v': lin(DEPTH, D, MEM_W),
        'mem_w_o': lin(DEPTH, MEM_W, D),
        'ffn_b_norm': gain(DEPTH, D),
        'ffn_b_wi': lin(DEPTH, D, 2 * D_FF),
        'ffn_b_wo': lin(DEPTH, D_FF, D),
    }


def reference(x_prompt, x_sample, cache_mem_k, cache_mem_v, state_gla, state_pool, cache_sb_k, cache_sb_v,
              page_table, mem_prompt, ffn_a_norm, ffn_a_wi, ffn_a_wo, mix_norm, gla_w_in, gla_w_gate, gla_b_gate,
              gla_out_norm, gla_w_out, sg_w_in, sg_v_norm, sg_w_s, sg_b_s, sg_w_out, pool_w, pool_scale, sb_w_qkv,
              sb_q_norm, sb_k_norm, sb_logit_bias, sb_w_out, mem_x_norm, mem_in_norm, mem_w_q, mem_q_norm, mem_w_k,
              mem_k_norm, mem_w_v, mem_w_o, ffn_b_norm, ffn_b_wi, ffn_b_wo):
    xp, xs = x_prompt, x_sample
    mem_k_new, mem_v_new = [], []
    gla_p, gla_s, pool_p, pool_s, sg_s = [], [], [], [], []
    sbk_p, sbv_p, sbk_s, sbv_s = [], [], [], []
    for i in range(DEPTH):
        kind, j = i % N_MIXERS, i // N_MIXERS
        xp = xp + 0.5 * swiglu(rms_norm(xp, ffn_a_norm[i]), ffn_a_wi[i], ffn_a_wo[i])
        xs = xs + 0.5 * swiglu(rms_norm(xs, ffn_a_norm[i]), ffn_a_wi[i], ffn_a_wo[i])
        hp, hs = rms_norm(xp, mix_norm[i]), rms_norm(xs, mix_norm[i])
        if kind == 0:
            s0 = jnp.zeros((hp.shape[0], GLA_HEADS, GLA_DK, GLA_DV), jnp.float32)
            dp, st_p = gla_mix(hp, s0, gla_w_in[j], gla_w_gate[j], gla_b_gate[j], gla_out_norm[j], gla_w_out[j])
            ds, st_s = gla_mix(hs, state_gla[j].astype(jnp.float32), gla_w_in[j], gla_w_gate[j], gla_b_gate[j],
                               gla_out_norm[j], gla_w_out[j])
            gla_p.append(st_p)
            gla_s.append(st_s)
        elif kind == 1:
            dp, _ = sg_mix(hp, sg_w_in[j], sg_v_norm[j], sg_w_s[j], sg_b_s[j], sg_w_out[j])
            ds, v_s = sg_mix(hs, sg_w_in[j], sg_v_norm[j], sg_w_s[j], sg_b_s[j], sg_w_out[j])
            sg_s.append(v_s)
        elif kind == 2:
            dp, buf_p = pool_mix(hp, hp[:, :0], pool_w[j], pool_scale[j])
            ds, buf_s = pool_mix(hs, state_pool[j], pool_w[j], pool_scale[j])
            pool_p.append(buf_p)
            pool_s.append(buf_s)
        else:
            Bp, Tp, _ = hp.shape
            Bs, Ts, _ = hs.shape
            qp, kp, vp = sb_project(hp, sb_w_qkv[j], sb_q_norm[j], sb_k_norm[j])
            dp = sb_attend(qp, kp, vp, sb_logit_bias[j], 0).reshape(Bp, Tp, D_MODEL) @ sb_w_out[j]
            qs, kn, vn = sb_project(hs, sb_w_qkv[j], sb_q_norm[j], sb_k_norm[j])
            k_past = cache_sb_k[j, page_table].reshape(Bs, -1, SB_HEADS, SB_HD).astype(kn.dtype)
            v_past = cache_sb_v[j, page_table].reshape(Bs, -1, SB_HEADS, SB_HD).astype(vn.dtype)
            k_all = jnp.concatenate([k_past, kn], axis=1)
            v_all = jnp.concatenate([v_past, vn], axis=1)
            ds = sb_attend(qs, k_all, v_all, sb_logit_bias[j], k_past.shape[1]).reshape(Bs, Ts, D_MODEL) @ sb_w_out[j]
            sbk_p.append(kp)
            sbv_p.append(vp)
            sbk_s.append(kn)
            sbv_s.append(vn)
        xp = xp + dp
        xs = xs + ds
        mk, mv = mem_kv(mem_prompt, mem_in_norm[i], mem_w_k[i], mem_k_norm[i], mem_w_v[i])
        mem_k_new.append(mk)
        mem_v_new.append(mv)
        xp = xp + mem_attend(rms_norm(xp, mem_x_norm[i]), mk, mv, mem_w_q[i], mem_q_norm[i], mem_w_o[i])
        xs = xs + mem_attend(rms_norm(xs, mem_x_norm[i]), cache_mem_k[i], cache_mem_v[i], mem_w_q[i],
                             mem_q_norm[i], mem_w_o[i])
        xp = xp + 0.5 * swiglu(rms_norm(xp, ffn_b_norm[i]), ffn_b_wi[i], ffn_b_wo[i])
        xs = xs + 0.5 * swiglu(rms_norm(xs, ffn_b_norm[i]), ffn_b_wi[i], ffn_b_wo[i])
    return (xp, xs, jnp.stack(mem_k_new), jnp.stack(mem_v_new), jnp.stack(gla_p), jnp.stack(gla_s),
            jnp.stack(pool_p), jnp.stack(pool_s), jnp.stack(sg_s), jnp.stack(sbk_p), jnp.stack(sbv_p),
            jnp.stack(sbk_s), jnp.stack(sbv_s))
```

```python
import functools
import math

import jax
import jax.numpy as jnp
from jax import lax
from jax.experimental import pallas as pl
from jax.experimental.pallas import tpu as pltpu

F32 = jnp.float32
BF16 = jnp.bfloat16

EPS = 1e-6
LANES = 128
SAMPLE_PAD = 8
VMEM_LIMIT_BYTES = 56 * 1024 * 1024

MEM_HEADS = 4
MEM_HD = 128
GLA_HEADS = 4
GLA_RANK = 16
GLA_TAU = 16.0
GLA_CHUNK = 64
SG_CHUNK = 128
SG_GROUPS = 8
POOL_WINDOWS = (2, 4, 8, 16)
POOL_BUF = max(POOL_WINDOWS) - 1
POOL_HALO = 16
SB_HEADS = 16
PAGE_SIZE = 128


def _params(*sem):
    return pltpu.CompilerParams(dimension_semantics=sem, vmem_limit_bytes=VMEM_LIMIT_BYTES)


def _rms(x, g):
    ms = jnp.mean(x * x, axis=-1, keepdims=True)
    return x * lax.rsqrt(ms + EPS) * g


def _bdot(a, b):
    return jnp.dot(a.astype(BF16), b.astype(BF16), preferred_element_type=F32)


def _bdot_nt(a, b):
    return lax.dot_general(a.astype(BF16), b.astype(BF16), (((1,), (1,)), ((), ())),
                           preferred_element_type=F32)


def _softplus_core(z):
    return jnp.log(1.0 + jnp.exp(-jnp.abs(z)))


def _split3(x):
    hi = x.astype(BF16)
    r1 = x - hi.astype(F32)
    mid = r1.astype(BF16)
    lo = (r1 - mid.astype(F32)).astype(BF16)
    return hi, mid, lo


def _ffn_body(x_ref, g_ref, wg_ref, wu_ref, wo_ref, o_ref, h_ref, *, n_ff):
    j = pl.program_id(1)

    @pl.when(j == 0)
    def _():
        h_ref[...] = _rms(x_ref[...], g_ref[...]).astype(BF16)
        o_ref[...] = jnp.zeros_like(o_ref)

    h = h_ref[...]
    g = jnp.dot(h, wg_ref[...].astype(BF16), preferred_element_type=F32)
    u = jnp.dot(h, wu_ref[...].astype(BF16), preferred_element_type=F32)
    a = (g * jax.nn.sigmoid(g) * u).astype(BF16)
    o_ref[...] += jnp.dot(a, wo_ref[...].astype(BF16), preferred_element_type=F32)

    @pl.when(j == n_ff - 1)
    def _():
        o_ref[...] = x_ref[...] + 0.5 * o_ref[...]


def _ffn(x, gain, wi, wo, *, tm, tf):
    M, D = x.shape
    F = wo.shape[0]
    n_ff = F // tf
    assert M % tm == 0 and F % tf == 0
    return pl.pallas_call(
        functools.partial(_ffn_body, n_ff=n_ff),
        grid=(M // tm, n_ff),
        in_specs=[
            pl.BlockSpec((tm, D), lambda i, j: (i, 0)),
            pl.BlockSpec((1, D), lambda i, j: (0, 0)),
            pl.BlockSpec((D, tf), lambda i, j: (0, j)),
            pl.BlockSpec((D, tf), lambda i, j: (0, j + n_ff)),
            pl.BlockSpec((tf, D), lambda i, j: (j, 0)),
        ],
        out_specs=pl.BlockSpec((tm, D), lambda i, j: (i, 0)),
        out_shape=jax.ShapeDtypeStruct((M, D), F32),
        scratch_shapes=[pltpu.VMEM((tm, D), BF16)],
        compiler_params=_params("parallel", "arbitrary"),
        name="ffn",
    )(x, gain.reshape(1, D), wi, wi, wo)


def _mm_body(*refs, has_gain, act, has_res, n_headnorm):
    it = iter(refs)
    x_ref = next(it)
    g_ref = next(it) if has_gain else None
    w_ref = next(it)
    hg_ref = next(it) if n_headnorm else None
    r_ref = next(it) if has_res else None
    o_ref = next(it)
    h_ref = next(it)
    j = pl.program_id(1)

    @pl.when(j == 0)
    def _():
        x = x_ref[...]
        if has_gain:
            x = _rms(x, g_ref[...])
        h_ref[...] = x.astype(BF16)

    y = jnp.dot(h_ref[...], w_ref[...].astype(BF16), preferred_element_type=F32)
    if act == "gelu":
        y = jax.nn.gelu(y, approximate=True)
    if has_res:
        y = r_ref[...] + y
    o_ref[...] = y

    if n_headnorm:
        @pl.when(j < n_headnorm)
        def _():
            tn = o_ref.shape[1]
            for s in range(tn // LANES):
                sl = slice(s * LANES, (s + 1) * LANES)
                o_ref[:, sl] = _rms(y[:, sl], hg_ref[:, sl])


def _mm(x, w, *, tm, tn, n_cols=None, gain=None, act=None, res=None, headnorm=None):
    M, K = x.shape
    N = n_cols if n_cols is not None else w.shape[1]
    assert M % tm == 0 and N % tn == 0
    args, specs = [x], [pl.BlockSpec((tm, K), lambda i, j: (i, 0))]
    if gain is not None:
        args.append(gain.reshape(1, K))
        specs.append(pl.BlockSpec((1, K), lambda i, j: (0, 0)))
    args.append(w)
    specs.append(pl.BlockSpec((K, tn), lambda i, j: (0, j)))
    n_headnorm = 0
    if headnorm is not None:
        hg, n_headnorm = headnorm
        args.append(hg)
        specs.append(pl.BlockSpec((1, tn), lambda i, j: (0, j)))
    if res is not None:
        args.append(res)
        specs.append(pl.BlockSpec((tm, tn), lambda i, j: (i, j)))
    return pl.pallas_call(
        functools.partial(_mm_body, has_gain=gain is not None, act=act, has_res=res is not None,
                          n_headnorm=n_headnorm),
        grid=(M // tm, N // tn),
        in_specs=specs,
        out_specs=pl.BlockSpec((tm, tn), lambda i, j: (i, j)),
        out_shape=jax.ShapeDtypeStruct((M, N), F32),
        scratch_shapes=[pltpu.VMEM((tm, K), BF16)],
        compiler_params=_params("parallel", "arbitrary"),
        name="mm",
    )(*args)


def _memattn_body(x_ref, gx_ref, wq_ref, gq_ref, k_ref, v_ref, wo_ref, o_ref):
    x = x_ref[...]
    h = _rms(x, gx_ref[...])
    q = _bdot(h, wq_ref[...])
    scale = MEM_HD ** -0.5
    outs = []
    for hd in range(MEM_HEADS):
        sl = slice(hd * MEM_HD, (hd + 1) * MEM_HD)
        qh = _rms(q[:, sl], gq_ref[...])
        s = _bdot_nt(qh, k_ref[:, sl]) * scale
        s = s - jnp.max(s, axis=-1, keepdims=True)
        e = jnp.exp(s)
        p = e / jnp.sum(e, axis=-1, keepdims=True)
        outs.append(_bdot(p, v_ref[:, sl]))
    o = jnp.concatenate(outs, axis=-1)
    o_ref[...] = x + _bdot(o, wo_ref[...])


def _mem_attend(x, gx, wq, gq, k, v, wo, *, rows_per_seq, tm):
    M, D = x.shape
    B, N, W = k.shape
    nt = rows_per_seq // tm
    assert rows_per_seq % tm == 0 and M == B * rows_per_seq
    return pl.pallas_call(
        _memattn_body,
        grid=(B, nt),
        in_specs=[
            pl.BlockSpec((tm, D), lambda b, i: (b * nt + i, 0)),
            pl.BlockSpec((1, D), lambda b, i: (0, 0)),
            pl.BlockSpec((D, W), lambda b, i: (0, 0)),
            pl.BlockSpec((1, MEM_HD), lambda b, i: (0, 0)),
            pl.BlockSpec((None, N, W), lambda b, i: (b, 0, 0)),
            pl.BlockSpec((None, N, W), lambda b, i: (b, 0, 0)),
            pl.BlockSpec((W, D), lambda b, i: (0, 0)),
        ],
        out_specs=pl.BlockSpec((tm, D), lambda b, i: (b * nt + i, 0)),
        out_shape=jax.ShapeDtypeStruct((M, D), F32),
        compiler_params=_params("parallel", "arbitrary"),
        name="mem_attend",
    )(x, gx.reshape(1, D), wq, gq.reshape(1, MEM_HD), k, v, wo)


def _gla_body(q_ref, k_ref, v_ref, r_ref, a_ref, wg_ref, bg_ref, go_ref, s0_ref, y_ref, s_ref,
              *, c, valid, dk, dv):
    i = pl.program_id(1)

    @pl.when(i == 0)
    def _():
        s_ref[...] = s0_ref[...]

    R = LANES
    rows = lax.broadcasted_iota(jnp.int32, (R, 1), 0)
    live = rows < valid
    tri_r = (lax.broadcasted_iota(jnp.int32, (R, R), 1) <= lax.broadcasted_iota(jnp.int32, (R, R), 0))
    tri_b = tri_r.astype(BF16)
    tri_c = tri_r[:c, :]

    def pad(x):
        return jnp.concatenate([x, jnp.zeros((R - c, x.shape[1]), F32)], axis=0)

    a = pad(a_ref[...])
    for hd in range(GLA_HEADS):
        ks = slice(hd * dk, (hd + 1) * dk)
        vs = slice(hd * dv, (hd + 1) * dv)
        la = _bdot(a, wg_ref[:, ks]) + bg_ref[:, ks]
        la = (jnp.minimum(la, 0.0) - _softplus_core(la)) / GLA_TAU
        la = jnp.where(live, la, 0.0)
        hi, mid, lo = _split3(la)
        b = (jnp.dot(tri_b, hi, preferred_element_type=F32)
             + jnp.dot(tri_b, mid, preferred_element_type=F32)
             + jnp.dot(tri_b, lo, preferred_element_type=F32))
        b_last = b[R - 1:R, :]
        q = q_ref[:, ks] * (dk ** -0.5)
        k = jnp.where(live, pad(k_ref[:, ks]), 0.0)
        v = jnp.where(live, pad(v_ref[:, vs]), 0.0)
        qd = q * jnp.exp(b[:c, :])
        kd = k * jnp.exp(-b)
        att = jnp.where(tri_c, _bdot_nt(qd, kd), 0.0)
        S = s_ref[0, hd]
        o = _bdot(att, v) + _bdot(qd, S)
        kdd = k * jnp.exp(b_last - b)
        stacked = jnp.where(rows < R - 1, kdd, jnp.exp(b_last))
        st = stacked.T
        decay = st[:, R - 1:R]
        s_ref[0, hd] = decay * S + _bdot(st, v)
        r = r_ref[:, vs]
        y_ref[:, vs] = _rms(o, go_ref[...]) * (r * jax.nn.sigmoid(r))


def _gla(z, a, w_gate_pad, b_gate, g_out, s0, *, rows_per_seq, c, valid):
    M = z.shape[0]
    B, H, dk, dv = s0.shape
    hk, hv = H * dk, H * dv
    n = rows_per_seq // c
    assert rows_per_seq % c == 0 and c < LANES and hv == 2 * hk
    body = functools.partial(_gla_body, c=c, valid=valid, dk=dk, dv=dv)
    return pl.pallas_call(
        body,
        grid=(B, n),
        in_specs=[
            pl.BlockSpec((c, hk), lambda b, i: (b * n + i, 0)),
            pl.BlockSpec((c, hk), lambda b, i: (b * n + i, 1)),
            pl.BlockSpec((c, hv), lambda b, i: (b * n + i, 1)),
            pl.BlockSpec((c, hv), lambda b, i: (b * n + i, 2)),
            pl.BlockSpec((c, LANES), lambda b, i: (b * n + i, 0)),
            pl.BlockSpec((LANES, hk), lambda b, i: (0, 0)),
            pl.BlockSpec((1, hk), lambda b, i: (0, 0)),
            pl.BlockSpec((1, dv), lambda b, i: (0, 0)),
            pl.BlockSpec((1, H, dk, dv), lambda b, i: (b, 0, 0, 0)),
        ],
        out_specs=[
            pl.BlockSpec((c, hv), lambda b, i: (b * n + i, 0)),
            pl.BlockSpec((1, H, dk, dv), lambda b, i: (b, 0, 0, 0)),
        ],
        out_shape=[jax.ShapeDtypeStruct((M, hv), F32), jax.ShapeDtypeStruct((B, H, dk, dv), F32)],
        compiler_params=_params("parallel", "arbitrary"),
        name="gla",
    )(z, z, z, z, a, w_gate_pad, b_gate.reshape(1, hk), g_out.reshape(1, dv), s0)


def _sg_body(u_ref, v_ref, gv_ref, ws_ref, bs_ref, vn_ref, y_ref, *, c, n_chunks, gw):
    R = ws_ref.shape[1]
    tri = (lax.broadcasted_iota(jnp.int32, (R, R), 1) <= lax.broadcasted_iota(jnp.int32, (R, R), 0))
    vn = _rms(v_ref[...], gv_ref[...])
    vn_ref[...] = vn
    for g in range(SG_GROUPS):
        w = jnp.where(tri, ws_ref[g], 0.0).astype(BF16)
        gs = slice(g * gw, (g + 1) * gw)
        for t in range(n_chunks):
            rs = slice(t * c, (t + 1) * c)
            vc = vn[rs, gs]
            if c < R:
                vc = jnp.concatenate([vc, jnp.zeros((R - c, gw), F32)], axis=0)
            mixed = jnp.dot(w, vc.astype(BF16), preferred_element_type=F32)[:c, :] + bs_ref[:, gs]
            y_ref[rs, gs] = u_ref[rs, gs] * mixed


def _sg(uv, g_v, w_s, b_full, *, c, n_chunks):
    M, W2 = uv.shape
    W = W2 // 2
    tm = c * n_chunks
    assert M % tm == 0
    body = functools.partial(_sg_body, c=c, n_chunks=n_chunks, gw=W // SG_GROUPS)
    return pl.pallas_call(
        body,
        grid=(M // tm,),
        in_specs=[
            pl.BlockSpec((tm, W), lambda i: (i, 0)),
            pl.BlockSpec((tm, W), lambda i: (i, 1)),
            pl.BlockSpec((1, W), lambda i: (0, 0)),
            pl.BlockSpec(w_s.shape, lambda i: (0, 0, 0)),
            pl.BlockSpec((c, W), lambda i: (0, 0)),
        ],
        out_specs=[pl.BlockSpec((tm, W), lambda i: (i, 0)), pl.BlockSpec((tm, W), lambda i: (i, 0))],
        out_shape=[jax.ShapeDtypeStruct((M, W), F32), jax.ShapeDtypeStruct((M, W), F32)],
        compiler_params=_params("parallel"),
        name="sg",
    )(uv, uv, g_v.reshape(1, W), w_s, b_full)


def _rmsnorm_body(x_ref, g_ref, o_ref):
    o_ref[...] = _rms(x_ref[...], g_ref[...])


def _rmsnorm(x, gain):
    M, D = x.shape
    return pl.pallas_call(
        _rmsnorm_body,
        grid=(1,),
        in_specs=[pl.BlockSpec((M, D), lambda i: (0, 0)), pl.BlockSpec((1, D), lambda i: (0, 0))],
        out_specs=pl.BlockSpec((M, D), lambda i: (0, 0)),
        out_shape=jax.ShapeDtypeStruct((M, D), F32),
        name="rmsnorm",
    )(x, gain.reshape(1, D))


def _pool_body(x_ref, halo_ref, res_ref, g_ref, wp_ref, sc_ref, h_ref, o_ref, *, tm, gw, normalize):
    i = pl.program_id(1)
    h, halo = x_ref[...], halo_ref[...]
    if normalize:
        h, halo = _rms(h, g_ref[...]), _rms(halo, g_ref[...])
    h_ref[...] = h
    halo = jnp.where(i > 0, halo, 0.0)
    hc = jnp.concatenate([halo, h], axis=0)
    pos = i * tm + lax.broadcasted_iota(jnp.int32, (tm, 1), 0)
    for g, w in enumerate(POOL_WINDOWS):
        gs = slice(g * gw, (g + 1) * gw)
        s = hc[:, gs]
        span, n = 1, s.shape[0]
        while span < w:
            s = s[span:, :] + s[:n - span, :]
            n -= span
            span *= 2
        win = s[n - tm:, :]
        cnt = jnp.minimum(pos + 1, w).astype(F32)
        pooled = win / cnt - h[:, gs]
        o_ref[:, gs] = res_ref[:, gs] + _bdot(pooled, wp_ref[g]) * sc_ref[:, gs]


def _pool(x, res, gain, w_pool, scale, *, rows_per_seq, tm, normalize):
    M, D = x.shape
    nt = rows_per_seq // tm
    hb = tm // POOL_HALO
    G = len(POOL_WINDOWS)
    gw = D // G
    assert rows_per_seq % tm == 0 and tm % POOL_HALO == 0
    body = functools.partial(_pool_body, tm=tm, gw=gw, normalize=normalize)
    return pl.pallas_call(
        body,
        grid=(M // rows_per_seq, nt),
        in_specs=[
            pl.BlockSpec((tm, D), lambda b, i: (b * nt + i, 0)),
            pl.BlockSpec((POOL_HALO, D), lambda b, i: (jnp.maximum((b * nt + i) * hb - 1, 0), 0)),
            pl.BlockSpec((tm, D), lambda b, i: (b * nt + i, 0)),
            pl.BlockSpec((1, D), lambda b, i: (0, 0)),
            pl.BlockSpec((G, gw, gw), lambda b, i: (0, 0, 0)),
            pl.BlockSpec((1, D), lambda b, i: (0, 0)),
        ],
        out_specs=[pl.BlockSpec((tm, D), lambda b, i: (b * nt + i, 0)),
                   pl.BlockSpec((tm, D), lambda b, i: (b * nt + i, 0))],
        out_shape=[jax.ShapeDtypeStruct((M, D), F32), jax.ShapeDtypeStruct((M, D), F32)],
        compiler_params=_params("parallel", "arbitrary"),
        name="pool",
    )(x, x, res, gain.reshape(1, D), w_pool, scale.reshape(1, D))


def _sb_body(bias_ref, q_ref, k_ref, v_ref, o_ref, acc_ref, carry_ref, *, tq, tk, scale):
    hd = pl.program_id(1)
    qi = pl.program_id(2)
    bias = bias_ref[hd]
    q = q_ref[...].astype(BF16)
    upper = (lax.broadcasted_iota(jnp.int32, (tk, tk), 0) >
             lax.broadcasted_iota(jnp.int32, (tk, tk), 1)).astype(BF16)
    qpos = qi * tq + lax.broadcasted_iota(jnp.int32, (tq, 1), 0)
    acc_ref[...] = jnp.zeros_like(acc_ref)
    carry_ref[...] = jnp.zeros_like(carry_ref)
    n_blk = (qi * tq + tq + tk - 1) // tk

    def step(it, _):
        start = pl.multiple_of((n_blk - 1 - it) * tk, tk)
        k = k_ref[pl.ds(start, tk), :]
        v = v_ref[pl.ds(start, tk), :]
        z = _bdot_nt(q, k) * scale + bias
        mask = (start + lax.broadcasted_iota(jnp.int32, (1, tk), 1)) < qpos
        core = _softplus_core(z)
        log_beta = jnp.minimum(z, 0.0) - core
        log_1m = jnp.where(mask, jnp.minimum(-z, 0.0) - core, 0.0)
        hi = log_1m.astype(BF16)
        lo = (log_1m - hi.astype(F32)).astype(BF16)
        later = (jnp.dot(hi, upper, preferred_element_type=F32)
                 + jnp.dot(lo, upper, preferred_element_type=F32))
        between = later + carry_ref[...]
        w = jnp.where(mask, jnp.exp(log_beta + between), 0.0)
        acc_ref[...] += _bdot(w, v)
        carry_ref[...] = between[:, 0:1] + log_1m[:, 0:1]
        return 0

    lax.fori_loop(0, n_blk, step, 0)
    o_ref[...] = acc_ref[...]


def _sb_prompt(qkv, bias, *, rows_per_seq, tq, tk):
    M, D3 = qkv.shape
    D = D3 // 3
    hd = D // SB_HEADS
    B = M // rows_per_seq
    nq = rows_per_seq // tq
    assert hd == LANES and rows_per_seq % tq == 0 and rows_per_seq % tk == 0
    body = functools.partial(_sb_body, tq=tq, tk=tk, scale=hd ** -0.5)
    return pl.pallas_call(
        body,
        grid=(B, SB_HEADS, nq),
        in_specs=[
            pl.BlockSpec(memory_space=pltpu.SMEM),
            pl.BlockSpec((tq, hd), lambda b, h, i: (b * nq + i, h)),
            pl.BlockSpec((rows_per_seq, hd), lambda b, h, i: (b, SB_HEADS + h)),
            pl.BlockSpec((rows_per_seq, hd), lambda b, h, i: (b, 2 * SB_HEADS + h)),
        ],
        out_specs=pl.BlockSpec((tq, hd), lambda b, h, i: (b * nq + i, h)),
        out_shape=jax.ShapeDtypeStruct((M, D), F32),
        scratch_shapes=[pltpu.VMEM((tq, hd), F32), pltpu.VMEM((tq, 1), F32)],
        compiler_params=_params("parallel", "parallel", "arbitrary"),
        name="sb_prompt",
    )(bias, qkv, qkv, qkv)


def _sbs_body(pt_ref, q_ref, bias_ref, k_ref, v_ref, o_ref, qm_ref, acc_ref, carry_ref, *, scale, n_pages):
    j = pl.program_id(1)
    D = q_ref.shape[1]
    hd = D // SB_HEADS

    @pl.when(j == 0)
    def _():
        head = lax.broadcasted_iota(jnp.int32, (SB_HEADS, D), 0)
        col = lax.broadcasted_iota(jnp.int32, (SB_HEADS, D), 1)
        q = jnp.broadcast_to(q_ref[0:1, :], (SB_HEADS, D))
        qm_ref[...] = jnp.where(col // hd == head, q, 0.0).astype(BF16)
        acc_ref[...] = jnp.zeros_like(acc_ref)
        carry_ref[...] = jnp.zeros_like(carry_ref)

    ps = k_ref.shape[0]
    upper = (lax.broadcasted_iota(jnp.int32, (ps, ps), 0) >
             lax.broadcasted_iota(jnp.int32, (ps, ps), 1)).astype(BF16)
    z = _bdot_nt(qm_ref[...], k_ref[...]) * scale + bias_ref[...]
    core = _softplus_core(z)
    log_beta = jnp.minimum(z, 0.0) - core
    log_1m = jnp.minimum(-z, 0.0) - core
    hi = log_1m.astype(BF16)
    lo = (log_1m - hi.astype(F32)).astype(BF16)
    later = jnp.dot(hi, upper, preferred_element_type=F32) + jnp.dot(lo, upper, preferred_element_type=F32)
    between = later + carry_ref[...]
    w = jnp.exp(log_beta + between)
    acc_ref[...] += _bdot(w, v_ref[...])
    carry_ref[...] = between[:, 0:1] + log_1m[:, 0:1]

    @pl.when(j == n_pages - 1)
    def _():
        acc = acc_ref[...]
        row = jnp.concatenate([acc[h:h + 1, h * hd:(h + 1) * hd] for h in range(SB_HEADS)], axis=1)
        o_ref[...] = jnp.broadcast_to(row, o_ref.shape)


def _sb_sample(q_pad, bias, cache_k, cache_v, page_table):
    M, D = q_pad.shape
    B, n_pages = page_table.shape
    n_pool, ps = cache_k.shape[0], cache_k.shape[1]
    kf = cache_k.reshape(n_pool, ps, D)
    vf = cache_v.reshape(n_pool, ps, D)
    bias_b = jnp.broadcast_to(bias.reshape(SB_HEADS, 1), (SB_HEADS, ps)).astype(F32)
    body = functools.partial(_sbs_body, scale=(D // SB_HEADS) ** -0.5, n_pages=n_pages)
    grid_spec = pltpu.PrefetchScalarGridSpec(
        num_scalar_prefetch=1,
        grid=(B, n_pages),
        in_specs=[
            pl.BlockSpec((SAMPLE_PAD, D), lambda b, j, pt: (b, 0)),
            pl.BlockSpec((SB_HEADS, ps), lambda b, j, pt: (0, 0)),
            pl.BlockSpec((None, ps, D), lambda b, j, pt: (pt[b, n_pages - 1 - j], 0, 0)),
            pl.BlockSpec((None, ps, D), lambda b, j, pt: (pt[b, n_pages - 1 - j], 0, 0)),
        ],
        out_specs=pl.BlockSpec((SAMPLE_PAD, D), lambda b, j, pt: (b, 0)),
        scratch_shapes=[pltpu.VMEM((SB_HEADS, D), BF16), pltpu.VMEM((SB_HEADS, D), F32),
                        pltpu.VMEM((SB_HEADS, 1), F32)],
    )
    return pl.pallas_call(
        body,
        grid_spec=grid_spec,
        out_shape=jax.ShapeDtypeStruct((M, D), F32),
        compiler_params=_params("parallel", "arbitrary"),
        name="sb_sample",
    )(page_table, q_pad, bias_b, kf, vf)


def _row_tile(m, want):
    t = min(m, want)
    while m % t:
        t //= 2
    return t


def kernel(x_prompt, x_sample, cache_mem_k, cache_mem_v, state_gla, state_pool, cache_sb_k, cache_sb_v, page_table, mem_prompt, ffn_a_norm, ffn_a_wi, ffn_a_wo, mix_norm, gla_w_in, gla_w_gate, gla_b_gate, gla_out_norm, gla_w_out, sg_w_in, sg_v_norm, sg_w_s, sg_b_s, sg_w_out, pool_w, pool_scale, sb_w_qkv, sb_q_norm, sb_k_norm, sb_logit_bias, sb_w_out, mem_x_norm, mem_in_norm, mem_w_q, mem_q_norm, mem_w_k, mem_k_norm, mem_w_v, mem_w_o, ffn_b_norm, ffn_b_wi, ffn_b_wo):
    Bp, T, D = x_prompt.shape
    Bs = x_sample.shape[0]
    depth = ffn_a_norm.shape[0]
    n_mem = mem_prompt.shape[1]
    mem_w = MEM_HEADS * MEM_HD
    P = SAMPLE_PAD
    Mp, Ms = Bp * T, Bs * P

    xp = x_prompt.reshape(Mp, D)
    xs = jnp.pad(x_sample, ((0, 0), (0, P - 1), (0, 0))).reshape(Ms, D)

    tmp = _row_tile(Mp, 1024)
    tf = 256
    tn = 512

    def first_rows(a):
        return a.reshape(Bs, P, -1)[:, 0]

    mem_flat = mem_prompt.reshape(Bp * n_mem, D)
    mem_k_new, mem_v_new = [], []
    gla_p, gla_s, pool_p, pool_s, sg_s = [], [], [], [], []
    sbk_p, sbv_p, sbk_s, sbv_s = [], [], [], []

    for i in range(depth):
        kind, j = i % 4, i // 4
        xp = _ffn(xp, ffn_a_norm[i], ffn_a_wi[i], ffn_a_wo[i], tm=tmp, tf=tf)
        xs = _ffn(xs, ffn_a_norm[i], ffn_a_wi[i], ffn_a_wo[i], tm=Ms, tf=tf)

        if kind == 0:
            H, dk = GLA_HEADS, gla_w_gate.shape[2] // GLA_HEADS
            hk = H * dk
            hv = gla_w_out.shape[1]
            n_main = 2 * hk + 2 * hv
            w_a = jnp.pad(gla_w_in[j][:, n_main:], ((0, 0), (0, LANES - GLA_RANK)))
            w_g = jnp.pad(gla_w_gate[j], ((0, LANES - GLA_RANK), (0, 0)))
            s0p = jnp.zeros((Bp, H, dk, hv // H), F32)
            outs = []
            for x, rows, c, valid, s0, tm in ((xp, T, GLA_CHUNK, GLA_CHUNK, s0p, tmp),
                                              (xs, P, P, 1, state_gla[j], Ms)):
                z = _mm(x, gla_w_in[j], tm=tm, tn=tn, n_cols=n_main, gain=mix_norm[i])
                a = _mm(x, w_a, tm=tm, tn=LANES, gain=mix_norm[i])
                y, st = _gla(z, a, w_g, gla_b_gate[j], gla_out_norm[j], s0, rows_per_seq=rows, c=c, valid=valid)
                outs.append((_mm(y, gla_w_out[j], tm=tm, tn=tn, res=x), st))
            (xp, st_p), (xs, st_s) = outs
            gla_p.append(st_p)
            gla_s.append(st_s)
        elif kind == 1:
            W = sg_w_out.shape[1]
            outs = []
            for x, c, nck, tm in ((xp, SG_CHUNK, 4, tmp), (xs, P, 1, Ms)):
                uv = _mm(x, sg_w_in[j], tm=tm, tn=tn, gain=mix_norm[i], act="gelu")
                b_full = jnp.repeat(sg_b_s[j][:, :c].T, W // SG_GROUPS, axis=1)
                w_s = jnp.pad(sg_w_s[j][:, :c, :c], ((0, 0), (0, SG_CHUNK - c), (0, SG_CHUNK - c)))
                vn, y = _sg(uv, sg_v_norm[j], w_s, b_full, c=c, n_chunks=nck)
                outs.append((_mm(y, sg_w_out[j], tm=tm, tn=tn, res=x), vn))
            (xp, _), (xs, vn_s) = outs
            sg_s.append(first_rows(vn_s)[:, None, :])
        elif kind == 2:
            hp, xp = _pool(xp, xp, mix_norm[i], pool_w[j], pool_scale[j], rows_per_seq=T,
                           tm=_row_tile(T, 512), normalize=True)
            pool_p.append(hp.reshape(Bp, T, D)[:, T - POOL_BUF:])
            assert POOL_BUF + 1 == POOL_HALO
            hs = first_rows(_rmsnorm(xs, mix_norm[i]))
            hc = jnp.concatenate([state_pool[j], hs[:, None, :]], axis=1)
            res = jnp.pad(first_rows(xs)[:, None, :], ((0, 0), (POOL_BUF, 0), (0, 0)))
            _, out = _pool(hc.reshape(Bs * POOL_HALO, D), res.reshape(Bs * POOL_HALO, D), mix_norm[i],
                           pool_w[j], pool_scale[j], rows_per_seq=POOL_HALO, tm=POOL_HALO, normalize=False)
            xs_new = out.reshape(Bs, POOL_HALO, D)[:, POOL_HALO - 1:]
            xs = jnp.pad(xs_new, ((0, 0), (0, P - 1), (0, 0))).reshape(Ms, D)
            pool_s.append(hc[:, POOL_HALO - POOL_BUF:])
        else:
            hd = D // SB_HEADS
            hg = jnp.concatenate([jnp.tile(sb_q_norm[j], SB_HEADS), jnp.tile(sb_k_norm[j], SB_HEADS),
                                  jnp.ones((D,), F32)]).reshape(1, 3 * D)
            n_norm = 2 * D // tn
            qkv_p = _mm(xp, sb_w_qkv[j], tm=tmp, tn=tn, gain=mix_norm[i], headnorm=(hg, n_norm))
            att_p = _sb_prompt(qkv_p, sb_logit_bias[j], rows_per_seq=T, tq=256, tk=256)
            xp = _mm(att_p, sb_w_out[j], tm=tmp, tn=tn, res=xp)
            sbk_p.append(qkv_p[:, D:2 * D].reshape(Bp, T, SB_HEADS, hd))
            sbv_p.append(qkv_p[:, 2 * D:].reshape(Bp, T, SB_HEADS, hd))
            qkv_s = _mm(xs, sb_w_qkv[j], tm=Ms, tn=tn, gain=mix_norm[i], headnorm=(hg, n_norm))
            att_s = _sb_sample(qkv_s[:, :D], sb_logit_bias[j], cache_sb_k[j], cache_sb_v[j], page_table)
            xs = _mm(att_s, sb_w_out[j], tm=Ms, tn=tn, res=xs)
            sbk_s.append(first_rows(qkv_s[:, D:2 * D]).reshape(Bs, 1, SB_HEADS, hd))
            sbv_s.append(first_rows(qkv_s[:, 2 * D:]).reshape(Bs, 1, SB_HEADS, hd))

        hg_k = jnp.tile(mem_k_norm[i], MEM_HEADS).reshape(1, mem_w)
        mk = _mm(mem_flat, mem_w_k[i], tm=Bp * n_mem, tn=mem_w, gain=mem_in_norm[i], headnorm=(hg_k, 1))
        mv = _mm(mem_flat, mem_w_v[i], tm=Bp * n_mem, tn=mem_w, gain=mem_in_norm[i])
        mem_k_new.append(mk.reshape(Bp, n_mem, MEM_HEADS, MEM_HD))
        mem_v_new.append(mv.reshape(Bp, n_mem, MEM_HEADS, MEM_HD))
        wq_b, wo_b = mem_w_q[i].astype(BF16), mem_w_o[i].astype(BF16)
        xp = _mem_attend(xp, mem_x_norm[i], wq_b, mem_q_norm[i], mk.reshape(Bp, n_mem, mem_w),
                         mv.reshape(Bp, n_mem, mem_w), wo_b, rows_per_seq=T, tm=_row_tile(T, 512))
        xs = _mem_attend(xs, mem_x_norm[i], wq_b, mem_q_norm[i], cache_mem_k[i].reshape(Bs, n_mem, mem_w),
                         cache_mem_v[i].reshape(Bs, n_mem, mem_w), wo_b, rows_per_seq=P, tm=P)

        xp = _ffn(xp, ffn_b_norm[i], ffn_b_wi[i], ffn_b_wo[i], tm=tmp, tf=tf)
        xs = _ffn(xs, ffn_b_norm[i], ffn_b_wi[i], ffn_b_wo[i], tm=Ms, tf=tf)

    return (xp.reshape(Bp, T, D), first_rows(xs)[:, None, :], jnp.stack(mem_k_new), jnp.stack(mem_v_new),
            jnp.stack(gla_p), jnp.stack(gla_s), jnp.stack(pool_p), jnp.stack(pool_s), jnp.stack(sg_s),
            jnp.stack(sbk_p), jnp.stack(sbv_p), jnp.stack(sbk_s), jnp.stack(sbv_s))
```

```python
import functools

import jax
import jax.numpy as jnp
from jax import lax
from jax.experimental import pallas as pl
from jax.experimental.pallas import tpu as pltpu

F32 = jnp.float32
BF16 = jnp.bfloat16

EPS = 1e-6
LANES = 128
SAMPLE_PAD = 8
VMEM_LIMIT_BYTES = 56 * 1024 * 1024

MEM_HEADS = 4
MEM_HD = 128
GLA_HEADS = 4
GLA_RANK = 16
GLA_TAU = 16.0
GLA_CHUNK = 64
SG_CHUNK = 128
SG_GROUPS = 8
POOL_WINDOWS = (2, 4, 8, 16)
POOL_BUF = max(POOL_WINDOWS) - 1
POOL_HALO = 16
SB_HEADS = 16
SB_HEADS_PER_STEP = 4
SB_PAGES_PER_STEP = 8


def _params(*sem):
    return pltpu.CompilerParams(dimension_semantics=sem, vmem_limit_bytes=VMEM_LIMIT_BYTES)


def _rms(x, g):
    ms = jnp.mean(x * x, axis=-1, keepdims=True)
    return x * lax.rsqrt(ms + EPS) * g


def _bdot(a, b):
    return jnp.dot(a.astype(BF16), b.astype(BF16), preferred_element_type=F32)


def _bdot_nt(a, b):
    return lax.dot_general(a.astype(BF16), b.astype(BF16), (((1,), (1,)), ((), ())),
                           preferred_element_type=F32)


def _softplus_core(z):
    return jnp.log(1.0 + jnp.exp(-jnp.abs(z)))


def _split2(x):
    hi = x.astype(BF16)
    return hi, (x - hi.astype(F32)).astype(BF16)


def _split3(x):
    hi = x.astype(BF16)
    r1 = x - hi.astype(F32)
    mid = r1.astype(BF16)
    lo = (r1 - mid.astype(F32)).astype(BF16)
    return hi, mid, lo


def _later_matrix(n):
    return (lax.broadcasted_iota(jnp.int32, (n, n), 0) >
            lax.broadcasted_iota(jnp.int32, (n, n), 1)).astype(BF16)


def _row(a):
    return a.reshape(1, a.shape[-1])


def _ffn_body(x_ref, xs_ref, g_ref, wg_ref, wu_ref, wo_ref, o_ref, os_ref, h_ref, hs_ref, *, n_ff):
    i = pl.program_id(0)
    j = pl.program_id(1)

    def half_step(x_ref, o_ref, h_ref):
        @pl.when(j == 0)
        def _():
            h_ref[...] = _rms(x_ref[...], g_ref[...]).astype(BF16)
            o_ref[...] = jnp.zeros_like(o_ref)

        h = h_ref[...]
        g = jnp.dot(h, wg_ref[...].astype(BF16), preferred_element_type=F32)
        u = jnp.dot(h, wu_ref[...].astype(BF16), preferred_element_type=F32)
        a = (g * jax.nn.sigmoid(g) * u).astype(BF16)
        o_ref[...] += jnp.dot(a, wo_ref[...].astype(BF16), preferred_element_type=F32)

        @pl.when(j == n_ff - 1)
        def _():
            o_ref[...] = x_ref[...] + 0.5 * o_ref[...]

    half_step(x_ref, o_ref, h_ref)

    @pl.when(i == 0)
    def _():
        half_step(xs_ref, os_ref, hs_ref)


def _ffn(x, xs, gain, wi, wo, layer, *, tm, tf):
    M, D = x.shape
    Ms = xs.shape[0]
    F = wo.shape[1]
    n_ff = F // tf
    assert M % tm == 0 and F % tf == 0
    return pl.pallas_call(
        functools.partial(_ffn_body, n_ff=n_ff),
        grid=(M // tm, n_ff),
        in_specs=[
            pl.BlockSpec((tm, D), lambda i, j: (i, 0)),
            pl.BlockSpec((Ms, D), lambda i, j: (0, 0)),
            pl.BlockSpec((1, D), lambda i, j: (0, 0)),
            pl.BlockSpec((None, D, tf), lambda i, j: (layer, 0, j)),
            pl.BlockSpec((None, D, tf), lambda i, j: (layer, 0, j + n_ff)),
            pl.BlockSpec((None, tf, D), lambda i, j: (layer, j, 0)),
        ],
        out_specs=[pl.BlockSpec((tm, D), lambda i, j: (i, 0)), pl.BlockSpec((Ms, D), lambda i, j: (0, 0))],
        out_shape=[jax.ShapeDtypeStruct((M, D), F32), jax.ShapeDtypeStruct((Ms, D), F32)],
        scratch_shapes=[pltpu.VMEM((tm, D), BF16), pltpu.VMEM((Ms, D), BF16)],
        compiler_params=_params("arbitrary", "arbitrary"),
        name="ffn",
    )(x, xs, _row(gain), wi, wi, wo)


def _mm_body(*refs, has_gain, act, has_res, n_headnorm):
    it = iter(refs)
    x_ref = next(it)
    g_ref = next(it) if has_gain else None
    w_ref = next(it)
    hg_ref = next(it) if n_headnorm else None
    r_ref = next(it) if has_res else None
    o_ref = next(it)
    h_ref = next(it)
    j = pl.program_id(1)

    @pl.when(j == 0)
    def _():
        x = x_ref[...]
        if has_gain:
            x = _rms(x, g_ref[...])
        h_ref[...] = x.astype(BF16)

    y = jnp.dot(h_ref[...], w_ref[...].astype(BF16), preferred_element_type=F32)
    if act == "gelu":
        y = jax.nn.gelu(y, approximate=True)
    if has_res:
        y = r_ref[...] + y
    o_ref[...] = y

    if n_headnorm:
        @pl.when(j < n_headnorm)
        def _():
            tn = o_ref.shape[1]
            for s in range(tn // LANES):
                sl = slice(s * LANES, (s + 1) * LANES)
                o_ref[:, sl] = _rms(y[:, sl], hg_ref[:, sl])


def _mm(x, w, layer, *, tm, tn, n_cols=None, gain=None, act=None, res=None, headnorm=None):
    M, K = x.shape
    N = n_cols if n_cols is not None else w.shape[2]
    assert M % tm == 0 and N % tn == 0
    args, specs = [x], [pl.BlockSpec((tm, K), lambda i, j: (i, 0))]
    if gain is not None:
        args.append(_row(gain))
        specs.append(pl.BlockSpec((1, K), lambda i, j: (0, 0)))
    args.append(w)
    specs.append(pl.BlockSpec((None, K, tn), lambda i, j: (layer, 0, j)))
    n_headnorm = 0
    if headnorm is not None:
        hg, n_headnorm = headnorm
        args.append(hg)
        specs.append(pl.BlockSpec((1, tn), lambda i, j: (0, j)))
    if res is not None:
        args.append(res)
        specs.append(pl.BlockSpec((tm, tn), lambda i, j: (i, j)))
    return pl.pallas_call(
        functools.partial(_mm_body, has_gain=gain is not None, act=act, has_res=res is not None,
                          n_headnorm=n_headnorm),
        grid=(M // tm, N // tn),
        in_specs=specs,
        out_specs=pl.BlockSpec((tm, tn), lambda i, j: (i, j)),
        out_shape=jax.ShapeDtypeStruct((M, N), F32),
        scratch_shapes=[pltpu.VMEM((tm, K), BF16)],
        compiler_params=_params("parallel", "arbitrary"),
        name="mm",
    )(*args)


def _memattn_body(x_ref, gx_ref, wq_ref, gq_ref, k_ref, v_ref, wo_ref, o_ref):
    x = x_ref[...]
    h = _rms(x, gx_ref[...])
    q = _bdot(h, wq_ref[...])
    scale = MEM_HD ** -0.5
    outs = []
    for hd in range(MEM_HEADS):
        sl = slice(hd * MEM_HD, (hd + 1) * MEM_HD)
        qh = _rms(q[:, sl], gq_ref[...])
        s = _bdot_nt(qh, k_ref[:, sl]) * scale
        s = s - jnp.max(s, axis=-1, keepdims=True)
        e = jnp.exp(s)
        p = e / jnp.sum(e, axis=-1, keepdims=True)
        outs.append(_bdot(p, v_ref[:, sl]))
    o = jnp.concatenate(outs, axis=-1)
    o_ref[...] = x + _bdot(o, wo_ref[...])


def _mem_attend(x, gx, wq, gq, k, v, wo, layer, *, rows_per_seq, tm):
    M, D = x.shape
    B, N, W = k.shape
    nt = rows_per_seq // tm
    assert rows_per_seq % tm == 0 and M == B * rows_per_seq
    return pl.pallas_call(
        _memattn_body,
        grid=(B, nt),
        in_specs=[
            pl.BlockSpec((tm, D), lambda b, i: (b * nt + i, 0)),
            pl.BlockSpec((1, D), lambda b, i: (0, 0)),
            pl.BlockSpec((None, D, W), lambda b, i: (layer, 0, 0)),
            pl.BlockSpec((1, MEM_HD), lambda b, i: (0, 0)),
            pl.BlockSpec((None, N, W), lambda b, i: (b, 0, 0)),
            pl.BlockSpec((None, N, W), lambda b, i: (b, 0, 0)),
            pl.BlockSpec((None, W, D), lambda b, i: (layer, 0, 0)),
        ],
        out_specs=pl.BlockSpec((tm, D), lambda b, i: (b * nt + i, 0)),
        out_shape=jax.ShapeDtypeStruct((M, D), F32),
        compiler_params=_params("parallel", "arbitrary"),
        name="mem_attend",
    )(x, _row(gx), wq, _row(gq), k, v, wo)


def _gla_body(q_ref, k_ref, v_ref, r_ref, a_ref, wg_ref, bg_ref, go_ref, s0_ref, y_ref, s_ref,
              *, c, valid, dk, dv):
    i = pl.program_id(1)

    @pl.when(i == 0)
    def _():
        s_ref[...] = s0_ref[...]

    R = LANES
    rows = lax.broadcasted_iota(jnp.int32, (R, 1), 0)
    live = rows < valid
    tri_r = (lax.broadcasted_iota(jnp.int32, (R, R), 1) <= lax.broadcasted_iota(jnp.int32, (R, R), 0))
    tri_b = tri_r.astype(BF16)
    tri_c = tri_r[:c, :]

    def pad(x):
        return jnp.concatenate([x, jnp.zeros((R - c, x.shape[1]), F32)], axis=0)

    a = pad(a_ref[...])
    for hd in range(GLA_HEADS):
        ks = slice(hd * dk, (hd + 1) * dk)
        vs = slice(hd * dv, (hd + 1) * dv)
        la = _bdot(a, wg_ref[:, ks]) + bg_ref[:, ks]
        la = (jnp.minimum(la, 0.0) - _softplus_core(la)) / GLA_TAU
        la = jnp.where(live, la, 0.0)
        hi, mid, lo = _split3(la)
        b = (jnp.dot(tri_b, hi, preferred_element_type=F32)
             + jnp.dot(tri_b, mid, preferred_element_type=F32)
             + jnp.dot(tri_b, lo, preferred_element_type=F32))
        b_last = b[R - 1:R, :]
        q = q_ref[:, ks] * (dk ** -0.5)
        k = jnp.where(live, pad(k_ref[:, ks]), 0.0)
        v = jnp.where(live, pad(v_ref[:, vs]), 0.0)
        qd = q * jnp.exp(b[:c, :])
        kd = k * jnp.exp(-b)
        att = jnp.where(tri_c, _bdot_nt(qd, kd), 0.0)
        S = s_ref[0, hd]
        o = _bdot(att, v) + _bdot(qd, S)
        kdd = k * jnp.exp(b_last - b)
        stacked = jnp.where(rows < R - 1, kdd, jnp.exp(b_last))
        st = stacked.T
        decay = st[:, R - 1:R]
        s_ref[0, hd] = decay * S + _bdot(st, v)
        r = r_ref[:, vs]
        y_ref[:, vs] = _rms(o, go_ref[...]) * (r * jax.nn.sigmoid(r))


def _gla(z, a, w_gate_pad, b_gate, g_out, s0, *, rows_per_seq, c, valid):
    M = z.shape[0]
    B, H, dk, dv = s0.shape
    hk, hv = H * dk, H * dv
    n = rows_per_seq // c
    assert rows_per_seq % c == 0 and c < LANES and hv == 2 * hk
    body = functools.partial(_gla_body, c=c, valid=valid, dk=dk, dv=dv)
    return pl.pallas_call(
        body,
        grid=(B, n),
        in_specs=[
            pl.BlockSpec((c, hk), lambda b, i: (b * n + i, 0)),
            pl.BlockSpec((c, hk), lambda b, i: (b * n + i, 1)),
            pl.BlockSpec((c, hv), lambda b, i: (b * n + i, 1)),
            pl.BlockSpec((c, hv), lambda b, i: (b * n + i, 2)),
            pl.BlockSpec((c, LANES), lambda b, i: (b * n + i, 0)),
            pl.BlockSpec((LANES, hk), lambda b, i: (0, 0)),
            pl.BlockSpec((1, hk), lambda b, i: (0, 0)),
            pl.BlockSpec((1, dv), lambda b, i: (0, 0)),
            pl.BlockSpec((1, H, dk, dv), lambda b, i: (b, 0, 0, 0)),
        ],
        out_specs=[
            pl.BlockSpec((c, hv), lambda b, i: (b * n + i, 0)),
            pl.BlockSpec((1, H, dk, dv), lambda b, i: (b, 0, 0, 0)),
        ],
        out_shape=[jax.ShapeDtypeStruct((M, hv), F32), jax.ShapeDtypeStruct((B, H, dk, dv), F32)],
        compiler_params=_params("parallel", "arbitrary"),
        name="gla",
    )(z, z, z, z, a, w_gate_pad, b_gate.reshape(1, hk), g_out.reshape(1, dv), s0)


def _sg_body(u_ref, v_ref, gv_ref, ws_ref, bs_ref, vn_ref, y_ref, *, c, n_chunks, gw):
    R = ws_ref.shape[1]
    tri = (lax.broadcasted_iota(jnp.int32, (R, R), 1) <= lax.broadcasted_iota(jnp.int32, (R, R), 0))
    vn = _rms(v_ref[...], gv_ref[...])
    vn_ref[...] = vn
    for g in range(SG_GROUPS):
        w = jnp.where(tri, ws_ref[g], 0.0).astype(BF16)
        gs = slice(g * gw, (g + 1) * gw)
        for t in range(n_chunks):
            rs = slice(t * c, (t + 1) * c)
            vc = vn[rs, gs]
            if c < R:
                vc = jnp.concatenate([vc, jnp.zeros((R - c, gw), F32)], axis=0)
            mixed = jnp.dot(w, vc.astype(BF16), preferred_element_type=F32)[:c, :] + bs_ref[:, gs]
            y_ref[rs, gs] = u_ref[rs, gs] * mixed


def _sg(uv, g_v, w_s, b_full, *, c, n_chunks):
    M, W2 = uv.shape
    W = W2 // 2
    tm = c * n_chunks
    assert M % tm == 0
    body = functools.partial(_sg_body, c=c, n_chunks=n_chunks, gw=W // SG_GROUPS)
    return pl.pallas_call(
        body,
        grid=(M // tm,),
        in_specs=[
            pl.BlockSpec((tm, W), lambda i: (i, 0)),
            pl.BlockSpec((tm, W), lambda i: (i, 1)),
            pl.BlockSpec((1, W), lambda i: (0, 0)),
            pl.BlockSpec(w_s.shape, lambda i: (0, 0, 0)),
            pl.BlockSpec((c, W), lambda i: (0, 0)),
        ],
        out_specs=[pl.BlockSpec((tm, W), lambda i: (i, 0)), pl.BlockSpec((tm, W), lambda i: (i, 0))],
        out_shape=[jax.ShapeDtypeStruct((M, W), F32), jax.ShapeDtypeStruct((M, W), F32)],
        compiler_params=_params("parallel"),
        name="sg",
    )(uv, uv, g_v.reshape(1, W), w_s, b_full)


def _rmsnorm_body(x_ref, g_ref, o_ref):
    o_ref[...] = _rms(x_ref[...], g_ref[...])


def _rmsnorm(x, gain):
    M, D = x.shape
    return pl.pallas_call(
        _rmsnorm_body,
        grid=(1,),
        in_specs=[pl.BlockSpec((M, D), lambda i: (0, 0)), pl.BlockSpec((1, D), lambda i: (0, 0))],
        out_specs=pl.BlockSpec((M, D), lambda i: (0, 0)),
        out_shape=jax.ShapeDtypeStruct((M, D), F32),
        name="rmsnorm",
    )(x, gain.reshape(1, D))


def _pool_body(x_ref, halo_ref, res_ref, g_ref, wp_ref, sc_ref, h_ref, o_ref, *, tm, gw, normalize):
    i = pl.program_id(1)
    h, halo = x_ref[...], halo_ref[...]
    if normalize:
        h, halo = _rms(h, g_ref[...]), _rms(halo, g_ref[...])
    h_ref[...] = h
    halo = jnp.where(i > 0, halo, 0.0)
    hc = jnp.concatenate([halo, h], axis=0)
    pos = i * tm + lax.broadcasted_iota(jnp.int32, (tm, 1), 0)
    for g, w in enumerate(POOL_WINDOWS):
        gs = slice(g * gw, (g + 1) * gw)
        s = hc[:, gs]
        span, n = 1, s.shape[0]
        while span < w:
            s = s[span:, :] + s[:n - span, :]
            n -= span
            span *= 2
        win = s[n - tm:, :]
        cnt = jnp.minimum(pos + 1, w).astype(F32)
        pooled = win / cnt - h[:, gs]
        o_ref[:, gs] = res_ref[:, gs] + _bdot(pooled, wp_ref[g]) * sc_ref[:, gs]


def _pool(x, res, gain, w_pool, scale, layer, *, rows_per_seq, tm, normalize):
    M, D = x.shape
    nt = rows_per_seq // tm
    hb = tm // POOL_HALO
    G = len(POOL_WINDOWS)
    gw = D // G
    assert rows_per_seq % tm == 0 and tm % POOL_HALO == 0
    body = functools.partial(_pool_body, tm=tm, gw=gw, normalize=normalize)
    return pl.pallas_call(
        body,
        grid=(M // rows_per_seq, nt),
        in_specs=[
            pl.BlockSpec((tm, D), lambda b, i: (b * nt + i, 0)),
            pl.BlockSpec((POOL_HALO, D), lambda b, i: (jnp.maximum((b * nt + i) * hb - 1, 0), 0)),
            pl.BlockSpec((tm, D), lambda b, i: (b * nt + i, 0)),
            pl.BlockSpec((1, D), lambda b, i: (0, 0)),
            pl.BlockSpec((None, G, gw, gw), lambda b, i: (layer, 0, 0, 0)),
            pl.BlockSpec((1, D), lambda b, i: (0, 0)),
        ],
        out_specs=[pl.BlockSpec((tm, D), lambda b, i: (b * nt + i, 0)),
                   pl.BlockSpec((tm, D), lambda b, i: (b * nt + i, 0))],
        out_shape=[jax.ShapeDtypeStruct((M, D), F32), jax.ShapeDtypeStruct((M, D), F32)],
        compiler_params=_params("parallel", "arbitrary"),
        name="pool",
    )(x, x, res, gain.reshape(1, D), w_pool, scale.reshape(1, D))


def _sb_body(bias_ref, q_ref, k_ref, v_ref, o_ref, carry_ref, *, t, nh, scale):
    hp = pl.program_id(1)
    qi = pl.program_id(2)
    later_m = _later_matrix(t)
    causal = (lax.broadcasted_iota(jnp.int32, (t, t), 1) < lax.broadcasted_iota(jnp.int32, (t, t), 0))

    def tile(start, diagonal):
        heads = [slice(h * LANES, (h + 1) * LANES) for h in range(nh)]
        zs = [_bdot_nt(q_ref[:, ls], k_ref[pl.ds(start, t), ls]) for ls in heads]
        log_betas, log_1ms = [], []
        for h, z in enumerate(zs):
            z = z * scale + bias_ref[hp * nh + h]
            core = _softplus_core(z)
            log_betas.append(jnp.minimum(z, 0.0) - core)
            log_1m = jnp.minimum(-z, 0.0) - core
            if diagonal:
                log_1m = jnp.where(causal, log_1m, 0.0)
            log_1ms.append(log_1m)
        laters = []
        for log_1m in log_1ms:
            both = jnp.dot(jnp.concatenate(_split2(log_1m), axis=0), later_m, preferred_element_type=F32)
            laters.append(both[:t, :] + both[t:, :])
        ws = []
        for h in range(nh):
            between = laters[h] if diagonal else laters[h] + carry_ref[h]
            carry_ref[h] = between[:, 0:1] + log_1ms[h][:, 0:1]
            w = jnp.exp(log_betas[h] + between)
            ws.append(jnp.where(causal, w, 0.0) if diagonal else w)
        for h, ls in enumerate(heads):
            pv = _bdot(ws[h], v_ref[pl.ds(start, t), ls])
            if diagonal:
                o_ref[:, ls] = pv
            else:
                o_ref[:, ls] += pv

    tile(pl.multiple_of(qi * t, t), True)

    def step(it, _):
        tile(pl.multiple_of((qi - 1 - it) * t, t), False)
        return 0

    lax.fori_loop(0, qi, step, 0)


def _sb_prompt(qkv, bias, *, rows_per_seq, t):
    M, D3 = qkv.shape
    D = D3 // 3
    hd = D // SB_HEADS
    nh = SB_HEADS_PER_STEP
    B = M // rows_per_seq
    nq = rows_per_seq // t
    groups = SB_HEADS // nh
    assert hd == LANES and rows_per_seq % t == 0 and SB_HEADS % nh == 0
    body = functools.partial(_sb_body, t=t, nh=nh, scale=hd ** -0.5)
    return pl.pallas_call(
        body,
        grid=(B, groups, nq),
        in_specs=[
            pl.BlockSpec(memory_space=pltpu.SMEM),
            pl.BlockSpec((t, nh * hd), lambda b, g, i: (b * nq + i, g)),
            pl.BlockSpec((rows_per_seq, nh * hd), lambda b, g, i: (b, groups + g)),
            pl.BlockSpec((rows_per_seq, nh * hd), lambda b, g, i: (b, 2 * groups + g)),
        ],
        out_specs=pl.BlockSpec((t, nh * hd), lambda b, g, i: (b * nq + i, g)),
        out_shape=jax.ShapeDtypeStruct((M, D), F32),
        scratch_shapes=[pltpu.VMEM((nh, t, 1), F32)],
        compiler_params=_params("parallel", "parallel", "arbitrary"),
        name="sb_prompt",
    )(bias, qkv, qkv, qkv)


def _sbs_body(pt_ref, q_ref, bias_ref, pick_ref, spread_ref, *refs, scale, n_steps, G):
    k_refs, v_refs = refs[:G], refs[G:2 * G]
    o_ref, acc_ref, carry_ref = refs[2 * G:]
    j = pl.program_id(1)
    H, hd = q_ref.shape
    ps = k_refs[0].shape[0]
    flat = ps * H

    @pl.when(j == 0)
    def _():
        acc_ref[...] = jnp.zeros_like(acc_ref)
        carry_ref[...] = jnp.zeros_like(carry_ref)

    own = (lax.broadcasted_iota(jnp.int32, (H, flat), 1) % H) == lax.broadcasted_iota(jnp.int32, (H, flat), 0)
    q = q_ref[...]
    nt = (((1,), (1,)), ((), ()))
    raw = [jnp.where(own, lax.dot_general(q, k_refs[g][...].reshape(flat, hd), nt,
                                          preferred_element_type=F32), 0.0) for g in range(G)]
    raw = jnp.concatenate(raw, axis=0)
    hi, mid, lo = _split3(raw)
    pick = pick_ref[...]
    z = (jnp.dot(hi, pick, preferred_element_type=F32) + jnp.dot(mid, pick, preferred_element_type=F32)
         + jnp.dot(lo, pick, preferred_element_type=F32))
    z = z * scale + bias_ref[...]
    core = _softplus_core(z)
    log_beta = jnp.minimum(z, 0.0) - core
    log_1m = jnp.minimum(-z, 0.0) - core
    l_hi, l_lo = _split2(log_1m)
    later_m = _later_matrix(ps)
    later = jnp.dot(l_hi, later_m, preferred_element_type=F32) + jnp.dot(l_lo, later_m, preferred_element_type=F32)
    total = later[:, 0:1] + log_1m[:, 0:1]
    carry = carry_ref[...]
    betweens = []
    for g in range(G):
        rs = slice(g * H, (g + 1) * H)
        betweens.append(later[rs, :] + carry)
        carry = carry + total[rs, :]
    carry_ref[...] = carry
    w = jnp.exp(log_beta + jnp.concatenate(betweens, axis=0)).astype(BF16)
    wide = jnp.dot(w, spread_ref[...], preferred_element_type=F32)
    acc = acc_ref[...]
    for g in range(G):
        wg = jnp.where(own, wide[g * H:(g + 1) * H, :], 0.0)
        acc = acc + jnp.dot(wg, v_refs[g][...].reshape(flat, hd), preferred_element_type=F32)
    acc_ref[...] = acc

    @pl.when(j == n_steps - 1)
    def _():
        o_ref[...] = acc


def _sb_sample(q, bias, cache_k, cache_v, page_table, layer):
    B, H, hd = q.shape
    n_pages = page_table.shape[1]
    ps = cache_k.shape[2]
    G = SB_PAGES_PER_STEP
    assert n_pages % G == 0
    n_steps = n_pages // G
    flat = ps * H
    pick = (jnp.arange(flat)[:, None] // H == jnp.arange(ps)[None, :]).astype(BF16)
    spread = pick.T
    bias_t = jnp.broadcast_to(jnp.tile(bias.astype(F32), G)[:, None], (G * H, ps))

    def page_spec(g):
        return pl.BlockSpec((None, None, ps, H, hd),
                            lambda b, j, pt: (layer, pt[b, n_pages - 1 - (j * G + g)], 0, 0, 0))

    body = functools.partial(_sbs_body, scale=hd ** -0.5, n_steps=n_steps, G=G)
    grid_spec = pltpu.PrefetchScalarGridSpec(
        num_scalar_prefetch=1,
        grid=(B, n_steps),
        in_specs=[
            pl.BlockSpec((None, H, hd), lambda b, j, pt: (b, 0, 0)),
            pl.BlockSpec((G * H, ps), lambda b, j, pt: (0, 0)),
            pl.BlockSpec((flat, ps), lambda b, j, pt: (0, 0)),
            pl.BlockSpec((ps, flat), lambda b, j, pt: (0, 0)),
        ] + [page_spec(g) for g in range(G)] + [page_spec(g) for g in range(G)],
        out_specs=pl.BlockSpec((None, H, hd), lambda b, j, pt: (b, 0, 0)),
        scratch_shapes=[pltpu.VMEM((H, hd), F32), pltpu.VMEM((H, 1), F32)],
    )
    return pl.pallas_call(
        body,
        grid_spec=grid_spec,
        out_shape=jax.ShapeDtypeStruct((B, H, hd), F32),
        compiler_params=_params("parallel", "arbitrary"),
        name="sb_sample",
    )(page_table, q, bias_t, pick, spread, *([cache_k] * G), *([cache_v] * G))


def _row_tile(m, want):
    t = min(m, want)
    while m % t:
        t //= 2
    return t


def kernel(x_prompt, x_sample, cache_mem_k, cache_mem_v, state_gla, state_pool, cache_sb_k, cache_sb_v, page_table, mem_prompt, ffn_a_norm, ffn_a_wi, ffn_a_wo, mix_norm, gla_w_in, gla_w_gate, gla_b_gate, gla_out_norm, gla_w_out, sg_w_in, sg_v_norm, sg_w_s, sg_b_s, sg_w_out, pool_w, pool_scale, sb_w_qkv, sb_q_norm, sb_k_norm, sb_logit_bias, sb_w_out, mem_x_norm, mem_in_norm, mem_w_q, mem_q_norm, mem_w_k, mem_k_norm, mem_w_v, mem_w_o, ffn_b_norm, ffn_b_wi, ffn_b_wo):
    Bp, T, D = x_prompt.shape
    Bs = x_sample.shape[0]
    depth = ffn_a_norm.shape[0]
    n_mem = mem_prompt.shape[1]
    mem_w = MEM_HEADS * MEM_HD
    P = SAMPLE_PAD
    Mp, Ms = Bp * T, Bs * P

    xp = x_prompt.reshape(Mp, D)
    xs = jnp.pad(x_sample, ((0, 0), (0, P - 1), (0, 0))).reshape(Ms, D)

    tmp = _row_tile(Mp, 1024)
    tf = 256
    tn = 512

    def first_rows(a):
        return a.reshape(Bs, P, -1)[:, 0]

    def pad_rows(a):
        return jnp.pad(a[:, None, :], ((0, 0), (0, P - 1), (0, 0))).reshape(Bs * P, -1)

    mem_flat = mem_prompt.reshape(Bp * n_mem, D)
    mem_wq_b, mem_wo_b = mem_w_q.astype(BF16), mem_w_o.astype(BF16)
    mem_k_new, mem_v_new = [], []
    gla_p, gla_s, pool_p, pool_s, sg_s = [], [], [], [], []
    sbk_p, sbv_p, sbk_s, sbv_s = [], [], [], []

    for i in range(depth):
        kind, j = i % 4, i // 4
        xp, xs = _ffn(xp, xs, ffn_a_norm[i], ffn_a_wi, ffn_a_wo, i, tm=tmp, tf=tf)

        if kind == 0:
            H, dk = GLA_HEADS, gla_w_gate.shape[2] // GLA_HEADS
            hk = H * dk
            hv = gla_w_out.shape[1]
            n_main = 2 * hk + 2 * hv
            w_a = jnp.pad(gla_w_in[j:j + 1, :, n_main:], ((0, 0), (0, 0), (0, LANES - GLA_RANK)))
            w_g = jnp.pad(gla_w_gate[j], ((0, LANES - GLA_RANK), (0, 0)))
            s0p = jnp.zeros((Bp, H, dk, hv // H), F32)
            outs = []
            for x, rows, c, valid, s0, tm in ((xp, T, GLA_CHUNK, GLA_CHUNK, s0p, tmp),
                                              (xs, P, P, 1, state_gla[j], Ms)):
                z = _mm(x, gla_w_in, j, tm=tm, tn=tn, n_cols=n_main, gain=mix_norm[i])
                a = _mm(x, w_a, 0, tm=tm, tn=LANES, gain=mix_norm[i])
                y, st = _gla(z, a, w_g, gla_b_gate[j], gla_out_norm[j], s0, rows_per_seq=rows, c=c, valid=valid)
                outs.append((_mm(y, gla_w_out, j, tm=tm, tn=tn, res=x), st))
            (xp, st_p), (xs, st_s) = outs
            gla_p.append(st_p)
            gla_s.append(st_s)
        elif kind == 1:
            W = sg_w_out.shape[1]
            outs = []
            for x, c, nck, tm in ((xp, SG_CHUNK, 4, tmp), (xs, P, 1, Ms)):
                uv = _mm(x, sg_w_in, j, tm=tm, tn=tn, gain=mix_norm[i], act="gelu")
                b_full = jnp.repeat(sg_b_s[j][:, :c].T, W // SG_GROUPS, axis=1)
                w_s = jnp.pad(sg_w_s[j][:, :c, :c], ((0, 0), (0, SG_CHUNK - c), (0, SG_CHUNK - c)))
                vn, y = _sg(uv, sg_v_norm[j], w_s, b_full, c=c, n_chunks=nck)
                outs.append((_mm(y, sg_w_out, j, tm=tm, tn=tn, res=x), vn))
            (xp, _), (xs, vn_s) = outs
            sg_s.append(first_rows(vn_s)[:, None, :])
        elif kind == 2:
            hp, xp = _pool(xp, xp, mix_norm[i], pool_w, pool_scale[j], j, rows_per_seq=T,
                           tm=_row_tile(T, 512), normalize=True)
            pool_p.append(hp.reshape(Bp, T, D)[:, T - POOL_BUF:])
            assert POOL_BUF + 1 == POOL_HALO
            hs = first_rows(_rmsnorm(xs, mix_norm[i]))
            hc = jnp.concatenate([state_pool[j], hs[:, None, :]], axis=1)
            res = jnp.pad(first_rows(xs)[:, None, :], ((0, 0), (POOL_BUF, 0), (0, 0)))
            _, out = _pool(hc.reshape(Bs * POOL_HALO, D), res.reshape(Bs * POOL_HALO, D), mix_norm[i],
                           pool_w, pool_scale[j], j, rows_per_seq=POOL_HALO, tm=POOL_HALO, normalize=False)
            xs = pad_rows(out.reshape(Bs, POOL_HALO, D)[:, POOL_HALO - 1])
            pool_s.append(hc[:, POOL_HALO - POOL_BUF:])
        else:
            hd = D // SB_HEADS
            hg = jnp.concatenate([jnp.tile(sb_q_norm[j], SB_HEADS), jnp.tile(sb_k_norm[j], SB_HEADS),
                                  jnp.ones((D,), F32)]).reshape(1, 3 * D)
            n_norm = 2 * D // tn
            qkv_p = _mm(xp, sb_w_qkv, j, tm=tmp, tn=tn, gain=mix_norm[i], headnorm=(hg, n_norm))
            att_p = _sb_prompt(qkv_p, sb_logit_bias[j], rows_per_seq=T, t=256)
            xp = _mm(att_p, sb_w_out, j, tm=tmp, tn=tn, res=xp)
            sbk_p.append(qkv_p[:, D:2 * D].reshape(Bp, T, SB_HEADS, hd))
            sbv_p.append(qkv_p[:, 2 * D:].reshape(Bp, T, SB_HEADS, hd))
            qkv_s = first_rows(_mm(xs, sb_w_qkv, j, tm=Ms, tn=tn, gain=mix_norm[i], headnorm=(hg, n_norm)))
            q_s, k_s, v_s = (qkv_s[:, n * D:(n + 1) * D].reshape(Bs, SB_HEADS, hd) for n in range(3))
            att_s = _sb_sample(q_s, sb_logit_bias[j], cache_sb_k, cache_sb_v, page_table, j)
            xs = _mm(pad_rows(att_s.reshape(Bs, D)), sb_w_out, j, tm=Ms, tn=tn, res=xs)
            sbk_s.append(k_s[:, None])
            sbv_s.append(v_s[:, None])

        hg_k = jnp.tile(mem_k_norm[i], MEM_HEADS).reshape(1, mem_w)
        mk = _mm(mem_flat, mem_w_k, i, tm=Bp * n_mem, tn=mem_w, gain=mem_in_norm[i], headnorm=(hg_k, 1))
        mv = _mm(mem_flat, mem_w_v, i, tm=Bp * n_mem, tn=mem_w, gain=mem_in_norm[i])
        mem_k_new.append(mk.reshape(Bp, n_mem, MEM_HEADS, MEM_HD))
        mem_v_new.append(mv.reshape(Bp, n_mem, MEM_HEADS, MEM_HD))
        xp = _mem_attend(xp, mem_x_norm[i], mem_wq_b, mem_q_norm[i], mk.reshape(Bp, n_mem, mem_w),
                         mv.reshape(Bp, n_mem, mem_w), mem_wo_b, i, rows_per_seq=T, tm=_row_tile(T, 512))
        xs = _mem_attend(xs, mem_x_norm[i], mem_wq_b, mem_q_norm[i], cache_mem_k[i].reshape(Bs, n_mem, mem_w),
                         cache_mem_v[i].reshape(Bs, n_mem, mem_w), mem_wo_b, i, rows_per_seq=P, tm=P)

        xp, xs = _ffn(xp, xs, ffn_b_norm[i], ffn_b_wi, ffn_b_wo, i, tm=tmp, tf=tf)

    return (xp.reshape(Bp, T, D), first_rows(xs)[:, None, :], jnp.stack(mem_k_new), jnp.stack(mem_v_new),
            jnp.stack(gla_p), jnp.stack(gla_s), jnp.stack(pool_p), jnp.stack(pool_s), jnp.stack(sg_s),
            jnp.stack(sbk_p), jnp.stack(sbv_p), jnp.stack(sbk_s), jnp.stack(sbv_s))
```

```python
import functools

import jax
import jax.numpy as jnp
from jax import lax
from jax.experimental import pallas as pl
from jax.experimental.pallas import tpu as pltpu

F32 = jnp.float32
BF16 = jnp.bfloat16

EPS = 1e-6
LANES = 128
SAMPLE_PAD = 8
VMEM_LIMIT_BYTES = 56 * 1024 * 1024

MEM_HEADS = 4
MEM_HD = 128
GLA_HEADS = 4
GLA_RANK = 16
GLA_TAU = 16.0
GLA_CHUNK = 64
SG_CHUNK = 128
SG_GROUPS = 8
POOL_WINDOWS = (2, 4, 8, 16)
POOL_BUF = max(POOL_WINDOWS) - 1
POOL_HALO = 16
SB_HEADS = 16
SB_HEADS_PER_STEP = 4
SB_PAGES_PER_STEP = 8


def _params(*sem):
    return pltpu.CompilerParams(dimension_semantics=sem, vmem_limit_bytes=VMEM_LIMIT_BYTES)


def _rms(x, g):
    ms = jnp.mean(x * x, axis=-1, keepdims=True)
    return x * lax.rsqrt(ms + EPS) * g


def _bdot(a, b):
    return jnp.dot(a.astype(BF16), b.astype(BF16), preferred_element_type=F32)


def _bdot_nt(a, b):
    return lax.dot_general(a.astype(BF16), b.astype(BF16), (((1,), (1,)), ((), ())),
                           preferred_element_type=F32)


def _softplus_core(z):
    return jnp.log(1.0 + jnp.exp(-jnp.abs(z)))


def _split2(x):
    hi = x.astype(BF16)
    return hi, (x - hi.astype(F32)).astype(BF16)


def _split3(x):
    hi = x.astype(BF16)
    r1 = x - hi.astype(F32)
    mid = r1.astype(BF16)
    lo = (r1 - mid.astype(F32)).astype(BF16)
    return hi, mid, lo


def _later_matrix(n):
    return (lax.broadcasted_iota(jnp.int32, (n, n), 0) >
            lax.broadcasted_iota(jnp.int32, (n, n), 1)).astype(BF16)


def _row(a):
    return a.reshape(1, a.shape[-1])


def _ffn_body(x_ref, xs_ref, g_ref, wg_ref, wu_ref, wo_ref, o_ref, os_ref, h_ref, hs_ref, *, n_ff):
    i = pl.program_id(0)
    j = pl.program_id(1)

    def half_step(x_ref, o_ref, h_ref):
        @pl.when(j == 0)
        def _():
            h_ref[...] = _rms(x_ref[...], g_ref[...]).astype(BF16)
            o_ref[...] = jnp.zeros_like(o_ref)

        h = h_ref[...]
        g = jnp.dot(h, wg_ref[...].astype(BF16), preferred_element_type=F32)
        u = jnp.dot(h, wu_ref[...].astype(BF16), preferred_element_type=F32)
        a = (g * jax.nn.sigmoid(g) * u).astype(BF16)
        o_ref[...] += jnp.dot(a, wo_ref[...].astype(BF16), preferred_element_type=F32)

        @pl.when(j == n_ff - 1)
        def _():
            o_ref[...] = x_ref[...] + 0.5 * o_ref[...]

    half_step(x_ref, o_ref, h_ref)

    @pl.when(i == 0)
    def _():
        half_step(xs_ref, os_ref, hs_ref)


def _ffn(x, xs, gain, wi, wo, layer, *, tm, tf):
    M, D = x.shape
    Ms = xs.shape[0]
    F = wo.shape[1]
    n_ff = F // tf
    assert M % tm == 0 and F % tf == 0
    return pl.pallas_call(
        functools.partial(_ffn_body, n_ff=n_ff),
        grid=(M // tm, n_ff),
        in_specs=[
            pl.BlockSpec((tm, D), lambda i, j: (i, 0)),
            pl.BlockSpec((Ms, D), lambda i, j: (0, 0)),
            pl.BlockSpec((1, D), lambda i, j: (0, 0)),
            pl.BlockSpec((None, D, tf), lambda i, j: (layer, 0, j)),
            pl.BlockSpec((None, D, tf), lambda i, j: (layer, 0, j + n_ff)),
            pl.BlockSpec((None, tf, D), lambda i, j: (layer, j, 0)),
        ],
        out_specs=[pl.BlockSpec((tm, D), lambda i, j: (i, 0)), pl.BlockSpec((Ms, D), lambda i, j: (0, 0))],
        out_shape=[jax.ShapeDtypeStruct((M, D), F32), jax.ShapeDtypeStruct((Ms, D), F32)],
        scratch_shapes=[pltpu.VMEM((tm, D), BF16), pltpu.VMEM((Ms, D), BF16)],
        compiler_params=_params("arbitrary", "arbitrary"),
        name="ffn",
    )(x, xs, _row(gain), wi, wi, wo)


def _mm_body(*refs, has_gain, act, has_res, n_headnorm):
    it = iter(refs)
    x_ref = next(it)
    g_ref = next(it) if has_gain else None
    w_ref = next(it)
    hg_ref = next(it) if n_headnorm else None
    r_ref = next(it) if has_res else None
    o_ref = next(it)
    h_ref = next(it)
    j = pl.program_id(1)

    @pl.when(j == 0)
    def _():
        x = x_ref[...]
        if has_gain:
            x = _rms(x, g_ref[...])
        h_ref[...] = x.astype(BF16)

    y = jnp.dot(h_ref[...], w_ref[...].astype(BF16), preferred_element_type=F32)
    if act == "gelu":
        y = jax.nn.gelu(y, approximate=True)
    if has_res:
        y = r_ref[...] + y
    o_ref[...] = y

    if n_headnorm:
        @pl.when(j < n_headnorm)
        def _():
            tn = o_ref.shape[1]
            for s in range(tn // LANES):
                sl = slice(s * LANES, (s + 1) * LANES)
                o_ref[:, sl] = _rms(y[:, sl], hg_ref[:, sl])


def _mm(x, w, layer, *, tm, tn, n_cols=None, gain=None, act=None, res=None, headnorm=None):
    M, K = x.shape
    N = n_cols if n_cols is not None else w.shape[2]
    assert M % tm == 0 and N % tn == 0
    args, specs = [x], [pl.BlockSpec((tm, K), lambda i, j: (i, 0))]
    if gain is not None:
        args.append(_row(gain))
        specs.append(pl.BlockSpec((1, K), lambda i, j: (0, 0)))
    args.append(w)
    specs.append(pl.BlockSpec((None, K, tn), lambda i, j: (layer, 0, j)))
    n_headnorm = 0
    if headnorm is not None:
        hg, n_headnorm = headnorm
        args.append(hg)
        specs.append(pl.BlockSpec((1, tn), lambda i, j: (0, j)))
    if res is not None:
        args.append(res)
        specs.append(pl.BlockSpec((tm, tn), lambda i, j: (i, j)))
    return pl.pallas_call(
        functools.partial(_mm_body, has_gain=gain is not None, act=act, has_res=res is not None,
                          n_headnorm=n_headnorm),
        grid=(M // tm, N // tn),
        in_specs=specs,
        out_specs=pl.BlockSpec((tm, tn), lambda i, j: (i, j)),
        out_shape=jax.ShapeDtypeStruct((M, N), F32),
        scratch_shapes=[pltpu.VMEM((tm, K), BF16)],
        compiler_params=_params("parallel", "arbitrary"),
        name="mm",
    )(*args)


def _memattn_body(x_ref, gx_ref, wq_ref, gq_ref, k_ref, v_ref, wo_ref, o_ref):
    x = x_ref[...]
    h = _rms(x, gx_ref[...])
    q = _bdot(h, wq_ref[...])
    scale = MEM_HD ** -0.5
    outs = []
    for hd in range(MEM_HEADS):
        sl = slice(hd * MEM_HD, (hd + 1) * MEM_HD)
        qh = _rms(q[:, sl], gq_ref[...])
        s = _bdot_nt(qh, k_ref[:, sl]) * scale
        s = s - jnp.max(s, axis=-1, keepdims=True)
        e = jnp.exp(s)
        p = e / jnp.sum(e, axis=-1, keepdims=True)
        outs.append(_bdot(p, v_ref[:, sl]))
    o = jnp.concatenate(outs, axis=-1)
    o_ref[...] = x + _bdot(o, wo_ref[...])


def _mem_attend(x, gx, wq, gq, k, v, wo, layer, *, rows_per_seq, tm):
    M, D = x.shape
    B, N, W = k.shape
    nt = rows_per_seq // tm
    assert rows_per_seq % tm == 0 and M == B * rows_per_seq
    return pl.pallas_call(
        _memattn_body,
        grid=(B, nt),
        in_specs=[
            pl.BlockSpec((tm, D), lambda b, i: (b * nt + i, 0)),
            pl.BlockSpec((1, D), lambda b, i: (0, 0)),
            pl.BlockSpec((None, D, W), lambda b, i: (layer, 0, 0)),
            pl.BlockSpec((1, MEM_HD), lambda b, i: (0, 0)),
            pl.BlockSpec((None, N, W), lambda b, i: (b, 0, 0)),
            pl.BlockSpec((None, N, W), lambda b, i: (b, 0, 0)),
            pl.BlockSpec((None, W, D), lambda b, i: (layer, 0, 0)),
        ],
        out_specs=pl.BlockSpec((tm, D), lambda b, i: (b * nt + i, 0)),
        out_shape=jax.ShapeDtypeStruct((M, D), F32),
        compiler_params=_params("parallel", "arbitrary"),
        name="mem_attend",
    )(x, _row(gx), wq, _row(gq), k, v, wo)


def _gla_body(q_ref, k_ref, v_ref, r_ref, a_ref, wg_ref, bg_ref, go_ref, s0_ref, y_ref, s_ref,
              *, c, valid, dk, dv):
    i = pl.program_id(1)

    @pl.when(i == 0)
    def _():
        s_ref[...] = s0_ref[...]

    R = LANES
    rows = lax.broadcasted_iota(jnp.int32, (R, 1), 0)
    live = rows < valid
    tri_r = (lax.broadcasted_iota(jnp.int32, (R, R), 1) <= lax.broadcasted_iota(jnp.int32, (R, R), 0))
    tri_b = tri_r.astype(BF16)
    tri_c = tri_r[:c, :]

    def pad(x):
        return jnp.concatenate([x, jnp.zeros((R - c, x.shape[1]), F32)], axis=0)

    a = pad(a_ref[...])
    heads = range(GLA_HEADS)
    kss = [slice(hd * dk, (hd + 1) * dk) for hd in heads]
    vss = [slice(hd * dv, (hd + 1) * dv) for hd in heads]
    las = [_bdot(a, wg_ref[:, ks]) + bg_ref[:, ks] for ks in kss]
    parts = []
    for la in las:
        la = (jnp.minimum(la, 0.0) - _softplus_core(la)) / GLA_TAU
        parts.append(jnp.concatenate(_split3(jnp.where(live, la, 0.0)), axis=1))
    bs = []
    for p in parts:
        c3 = jnp.dot(tri_b, p, preferred_element_type=F32)
        bs.append(c3[:, :dk] + c3[:, dk:2 * dk] + c3[:, 2 * dk:])
    ks_, vs_, qds, kds, sts = [], [], [], [], []
    for hd in heads:
        b = bs[hd]
        b_last = b[R - 1:R, :]
        k = jnp.where(live, pad(k_ref[:, kss[hd]]), 0.0)
        vs_.append(jnp.where(live, pad(v_ref[:, vss[hd]]), 0.0))
        qds.append(q_ref[:, kss[hd]] * (dk ** -0.5) * jnp.exp(b[:c, :]))
        kds.append(k * jnp.exp(-b))
        sts.append(jnp.where(rows < R - 1, k * jnp.exp(b_last - b), jnp.exp(b_last)))
    atts = [jnp.where(tri_c, _bdot_nt(qds[hd], kds[hd]), 0.0) for hd in heads]
    states = [s_ref[0, hd] for hd in heads]
    os_ = [_bdot(atts[hd], vs_[hd]) + _bdot(qds[hd], states[hd]) for hd in heads]
    for hd in heads:
        st = sts[hd].T
        s_ref[0, hd] = st[:, R - 1:R] * states[hd] + _bdot(st, vs_[hd])
    for hd in heads:
        r = r_ref[:, vss[hd]]
        y_ref[:, vss[hd]] = _rms(os_[hd], go_ref[...]) * (r * jax.nn.sigmoid(r))


def _gla(z, a, w_gate_pad, b_gate, g_out, s0, *, rows_per_seq, c, valid):
    M = z.shape[0]
    B, H, dk, dv = s0.shape
    hk, hv = H * dk, H * dv
    n = rows_per_seq // c
    assert rows_per_seq % c == 0 and c < LANES and hv == 2 * hk
    body = functools.partial(_gla_body, c=c, valid=valid, dk=dk, dv=dv)
    return pl.pallas_call(
        body,
        grid=(B, n),
        in_specs=[
            pl.BlockSpec((c, hk), lambda b, i: (b * n + i, 0)),
            pl.BlockSpec((c, hk), lambda b, i: (b * n + i, 1)),
            pl.BlockSpec((c, hv), lambda b, i: (b * n + i, 1)),
            pl.BlockSpec((c, hv), lambda b, i: (b * n + i, 2)),
            pl.BlockSpec((c, LANES), lambda b, i: (b * n + i, 0)),
            pl.BlockSpec((LANES, hk), lambda b, i: (0, 0)),
            pl.BlockSpec((1, hk), lambda b, i: (0, 0)),
            pl.BlockSpec((1, dv), lambda b, i: (0, 0)),
            pl.BlockSpec((1, H, dk, dv), lambda b, i: (b, 0, 0, 0)),
        ],
        out_specs=[
            pl.BlockSpec((c, hv), lambda b, i: (b * n + i, 0)),
            pl.BlockSpec((1, H, dk, dv), lambda b, i: (b, 0, 0, 0)),
        ],
        out_shape=[jax.ShapeDtypeStruct((M, hv), F32), jax.ShapeDtypeStruct((B, H, dk, dv), F32)],
        compiler_params=_params("parallel", "arbitrary"),
        name="gla",
    )(z, z, z, z, a, w_gate_pad, b_gate.reshape(1, hk), g_out.reshape(1, dv), s0)


def _sg_body(u_ref, v_ref, gv_ref, ws_ref, bs_ref, vn_ref, y_ref, *, c, n_chunks, gw):
    R = ws_ref.shape[1]
    tri = (lax.broadcasted_iota(jnp.int32, (R, R), 1) <= lax.broadcasted_iota(jnp.int32, (R, R), 0))
    vn = _rms(v_ref[...], gv_ref[...])
    vn_ref[...] = vn
    for g in range(SG_GROUPS):
        w = jnp.where(tri, ws_ref[g], 0.0).astype(BF16)
        gs = slice(g * gw, (g + 1) * gw)
        for t in range(n_chunks):
            rs = slice(t * c, (t + 1) * c)
            vc = vn[rs, gs]
            if c < R:
                vc = jnp.concatenate([vc, jnp.zeros((R - c, gw), F32)], axis=0)
            mixed = jnp.dot(w, vc.astype(BF16), preferred_element_type=F32)[:c, :] + bs_ref[:, gs]
            y_ref[rs, gs] = u_ref[rs, gs] * mixed


def _sg(uv, g_v, w_s, b_full, *, c, n_chunks):
    M, W2 = uv.shape
    W = W2 // 2
    tm = c * n_chunks
    assert M % tm == 0
    body = functools.partial(_sg_body, c=c, n_chunks=n_chunks, gw=W // SG_GROUPS)
    return pl.pallas_call(
        body,
        grid=(M // tm,),
        in_specs=[
            pl.BlockSpec((tm, W), lambda i: (i, 0)),
            pl.BlockSpec((tm, W), lambda i: (i, 1)),
            pl.BlockSpec((1, W), lambda i: (0, 0)),
            pl.BlockSpec(w_s.shape, lambda i: (0, 0, 0)),
            pl.BlockSpec((c, W), lambda i: (0, 0)),
        ],
        out_specs=[pl.BlockSpec((tm, W), lambda i: (i, 0)), pl.BlockSpec((tm, W), lambda i: (i, 0))],
        out_shape=[jax.ShapeDtypeStruct((M, W), F32), jax.ShapeDtypeStruct((M, W), F32)],
        compiler_params=_params("parallel"),
        name="sg",
    )(uv, uv, g_v.reshape(1, W), w_s, b_full)


def _rmsnorm_body(x_ref, g_ref, o_ref):
    o_ref[...] = _rms(x_ref[...], g_ref[...])


def _rmsnorm(x, gain):
    M, D = x.shape
    return pl.pallas_call(
        _rmsnorm_body,
        grid=(1,),
        in_specs=[pl.BlockSpec((M, D), lambda i: (0, 0)), pl.BlockSpec((1, D), lambda i: (0, 0))],
        out_specs=pl.BlockSpec((M, D), lambda i: (0, 0)),
        out_shape=jax.ShapeDtypeStruct((M, D), F32),
        name="rmsnorm",
    )(x, gain.reshape(1, D))


def _pool_body(x_ref, halo_ref, res_ref, g_ref, wp_ref, sc_ref, h_ref, o_ref, *, tm, gw, normalize):
    i = pl.program_id(1)
    h, halo = x_ref[...], halo_ref[...]
    if normalize:
        h, halo = _rms(h, g_ref[...]), _rms(halo, g_ref[...])
    h_ref[...] = h
    halo = jnp.where(i > 0, halo, 0.0)
    hc = jnp.concatenate([halo, h], axis=0)
    pos = i * tm + lax.broadcasted_iota(jnp.int32, (tm, 1), 0)
    for g, w in enumerate(POOL_WINDOWS):
        gs = slice(g * gw, (g + 1) * gw)
        s = hc[:, gs]
        span, n = 1, s.shape[0]
        while span < w:
            s = s[span:, :] + s[:n - span, :]
            n -= span
            span *= 2
        win = s[n - tm:, :]
        cnt = jnp.minimum(pos + 1, w).astype(F32)
        pooled = win / cnt - h[:, gs]
        o_ref[:, gs] = res_ref[:, gs] + _bdot(pooled, wp_ref[g]) * sc_ref[:, gs]


def _pool(x, res, gain, w_pool, scale, layer, *, rows_per_seq, tm, normalize):
    M, D = x.shape
    nt = rows_per_seq // tm
    hb = tm // POOL_HALO
    G = len(POOL_WINDOWS)
    gw = D // G
    assert rows_per_seq % tm == 0 and tm % POOL_HALO == 0
    body = functools.partial(_pool_body, tm=tm, gw=gw, normalize=normalize)
    return pl.pallas_call(
        body,
        grid=(M // rows_per_seq, nt),
        in_specs=[
            pl.BlockSpec((tm, D), lambda b, i: (b * nt + i, 0)),
            pl.BlockSpec((POOL_HALO, D), lambda b, i: (jnp.maximum((b * nt + i) * hb - 1, 0), 0)),
            pl.BlockSpec((tm, D), lambda b, i: (b * nt + i, 0)),
            pl.BlockSpec((1, D), lambda b, i: (0, 0)),
            pl.BlockSpec((None, G, gw, gw), lambda b, i: (layer, 0, 0, 0)),
            pl.BlockSpec((1, D), lambda b, i: (0, 0)),
        ],
        out_specs=[pl.BlockSpec((tm, D), lambda b, i: (b * nt + i, 0)),
                   pl.BlockSpec((tm, D), lambda b, i: (b * nt + i, 0))],
        out_shape=[jax.ShapeDtypeStruct((M, D), F32), jax.ShapeDtypeStruct((M, D), F32)],
        compiler_params=_params("parallel", "arbitrary"),
        name="pool",
    )(x, x, res, gain.reshape(1, D), w_pool, scale.reshape(1, D))


def _sb_body(bias_ref, q_ref, k_ref, v_ref, o_ref, carry_ref, *, t, nh, scale):
    hp = pl.program_id(1)
    qi = pl.program_id(2)
    later_m = _later_matrix(t)
    causal = (lax.broadcasted_iota(jnp.int32, (t, t), 1) < lax.broadcasted_iota(jnp.int32, (t, t), 0))

    def tile(start, diagonal):
        heads = [slice(h * LANES, (h + 1) * LANES) for h in range(nh)]
        zs = [_bdot_nt(q_ref[:, ls], k_ref[pl.ds(start, t), ls]) for ls in heads]
        log_betas, log_1ms = [], []
        for h, z in enumerate(zs):
            z = z * scale + bias_ref[hp * nh + h]
            core = _softplus_core(z)
            log_betas.append(jnp.minimum(z, 0.0) - core)
            log_1m = jnp.minimum(-z, 0.0) - core
            if diagonal:
                log_1m = jnp.where(causal, log_1m, 0.0)
            log_1ms.append(log_1m)
        laters = []
        for log_1m in log_1ms:
            laters.append(jnp.dot(log_1m.astype(BF16), later_m, preferred_element_type=F32))
        ws = []
        for h in range(nh):
            between = laters[h] if diagonal else laters[h] + carry_ref[h]
            carry_ref[h] = between[:, 0:1] + log_1ms[h][:, 0:1]
            w = jnp.exp(log_betas[h] + between)
            ws.append(jnp.where(causal, w, 0.0) if diagonal else w)
        for h, ls in enumerate(heads):
            pv = _bdot(ws[h], v_ref[pl.ds(start, t), ls])
            if diagonal:
                o_ref[:, ls] = pv
            else:
                o_ref[:, ls] += pv

    tile(pl.multiple_of(qi * t, t), True)

    def step(it, _):
        tile(pl.multiple_of((qi - 1 - it) * t, t), False)
        return 0

    lax.fori_loop(0, qi, step, 0)


def _sb_prompt(qkv, bias, *, rows_per_seq, t):
    M, D3 = qkv.shape
    D = D3 // 3
    hd = D // SB_HEADS
    nh = SB_HEADS_PER_STEP
    B = M // rows_per_seq
    nq = rows_per_seq // t
    groups = SB_HEADS // nh
    assert hd == LANES and rows_per_seq % t == 0 and SB_HEADS % nh == 0
    body = functools.partial(_sb_body, t=t, nh=nh, scale=hd ** -0.5)
    return pl.pallas_call(
        body,
        grid=(B, groups, nq),
        in_specs=[
            pl.BlockSpec(memory_space=pltpu.SMEM),
            pl.BlockSpec((t, nh * hd), lambda b, g, i: (b * nq + i, g)),
            pl.BlockSpec((rows_per_seq, nh * hd), lambda b, g, i: (b, groups + g)),
            pl.BlockSpec((rows_per_seq, nh * hd), lambda b, g, i: (b, 2 * groups + g)),
        ],
        out_specs=pl.BlockSpec((t, nh * hd), lambda b, g, i: (b * nq + i, g)),
        out_shape=jax.ShapeDtypeStruct((M, D), F32),
        scratch_shapes=[pltpu.VMEM((nh, t, 1), F32)],
        compiler_params=_params("parallel", "parallel", "arbitrary"),
        name="sb_prompt",
    )(bias, qkv, qkv, qkv)


def _sbs_body(pt_ref, q_ref, bias_ref, pick_ref, spread_ref, *refs, scale, n_steps, G):
    k_refs, v_refs = refs[:G], refs[G:2 * G]
    o_ref, acc_ref, carry_ref = refs[2 * G:]
    j = pl.program_id(1)
    H, hd = q_ref.shape
    ps = k_refs[0].shape[0]
    flat = ps * H

    @pl.when(j == 0)
    def _():
        acc_ref[...] = jnp.zeros_like(acc_ref)
        carry_ref[...] = jnp.zeros_like(carry_ref)

    own = (lax.broadcasted_iota(jnp.int32, (H, flat), 1) % H) == lax.broadcasted_iota(jnp.int32, (H, flat), 0)
    q = q_ref[...]
    nt = (((1,), (1,)), ((), ()))
    raw = [jnp.where(own, lax.dot_general(q, k_refs[g][...].reshape(flat, hd), nt,
                                          preferred_element_type=F32), 0.0) for g in range(G)]
    raw = jnp.concatenate(raw, axis=0)
    hi, mid, lo = _split3(raw)
    pick = pick_ref[...]
    z = (jnp.dot(hi, pick, preferred_element_type=F32) + jnp.dot(mid, pick, preferred_element_type=F32)
         + jnp.dot(lo, pick, preferred_element_type=F32))
    z = z * scale + bias_ref[...]
    core = _softplus_core(z)
    log_beta = jnp.minimum(z, 0.0) - core
    log_1m = jnp.minimum(-z, 0.0) - core
    l_hi, l_lo = _split2(log_1m)
    later_m = _later_matrix(ps)
    later = jnp.dot(l_hi, later_m, preferred_element_type=F32) + jnp.dot(l_lo, later_m, preferred_element_type=F32)
    total = later[:, 0:1] + log_1m[:, 0:1]
    carry = carry_ref[...]
    betweens = []
    for g in range(G):
        rs = slice(g * H, (g + 1) * H)
        betweens.append(later[rs, :] + carry)
        carry = carry + total[rs, :]
    carry_ref[...] = carry
    w = jnp.exp(log_beta + jnp.concatenate(betweens, axis=0)).astype(BF16)
    wide = jnp.dot(w, spread_ref[...], preferred_element_type=F32)
    acc = acc_ref[...]
    for g in range(G):
        wg = jnp.where(own, wide[g * H:(g + 1) * H, :], 0.0)
        acc = acc + jnp.dot(wg, v_refs[g][...].reshape(flat, hd), preferred_element_type=F32)
    acc_ref[...] = acc

    @pl.when(j == n_steps - 1)
    def _():
        o_ref[...] = acc


def _sb_sample(q, bias, cache_k, cache_v, page_table, layer):
    B, H, hd = q.shape
    n_pages = page_table.shape[1]
    ps = cache_k.shape[2]
    G = SB_PAGES_PER_STEP
    assert n_pages % G == 0
    n_steps = n_pages // G
    flat = ps * H
    pick = (jnp.arange(flat)[:, None] // H == jnp.arange(ps)[None, :]).astype(BF16)
    spread = pick.T
    bias_t = jnp.broadcast_to(jnp.tile(bias.astype(F32), G)[:, None], (G * H, ps))

    def page_spec(g):
        return pl.BlockSpec((None, None, ps, H, hd),
                            lambda b, j, pt: (layer, pt[b, n_pages - 1 - (j * G + g)], 0, 0, 0))

    body = functools.partial(_sbs_body, scale=hd ** -0.5, n_steps=n_steps, G=G)
    grid_spec = pltpu.PrefetchScalarGridSpec(
        num_scalar_prefetch=1,
        grid=(B, n_steps),
        in_specs=[
            pl.BlockSpec((None, H, hd), lambda b, j, pt: (b, 0, 0)),
            pl.BlockSpec((G * H, ps), lambda b, j, pt: (0, 0)),
            pl.BlockSpec((flat, ps), lambda b, j, pt: (0, 0)),
            pl.BlockSpec((ps, flat), lambda b, j, pt: (0, 0)),
        ] + [page_spec(g) for g in range(G)] + [page_spec(g) for g in range(G)],
        out_specs=pl.BlockSpec((None, H, hd), lambda b, j, pt: (b, 0, 0)),
        scratch_shapes=[pltpu.VMEM((H, hd), F32), pltpu.VMEM((H, 1), F32)],
    )
    return pl.pallas_call(
        body,
        grid_spec=grid_spec,
        out_shape=jax.ShapeDtypeStruct((B, H, hd), F32),
        compiler_params=_params("parallel", "arbitrary"),
        name="sb_sample",
    )(page_table, q, bias_t, pick, spread, *([cache_k] * G), *([cache_v] * G))


def _row_tile(m, want):
    t = min(m, want)
    while m % t:
        t //= 2
    return t


def kernel(x_prompt, x_sample, cache_mem_k, cache_mem_v, state_gla, state_pool, cache_sb_k, cache_sb_v, page_table, mem_prompt, ffn_a_norm, ffn_a_wi, ffn_a_wo, mix_norm, gla_w_in, gla_w_gate, gla_b_gate, gla_out_norm, gla_w_out, sg_w_in, sg_v_norm, sg_w_s, sg_b_s, sg_w_out, pool_w, pool_scale, sb_w_qkv, sb_q_norm, sb_k_norm, sb_logit_bias, sb_w_out, mem_x_norm, mem_in_norm, mem_w_q, mem_q_norm, mem_w_k, mem_k_norm, mem_w_v, mem_w_o, ffn_b_norm, ffn_b_wi, ffn_b_wo):
    Bp, T, D = x_prompt.shape
    Bs = x_sample.shape[0]
    depth = ffn_a_norm.shape[0]
    n_mem = mem_prompt.shape[1]
    mem_w = MEM_HEADS * MEM_HD
    P = SAMPLE_PAD
    Mp, Ms = Bp * T, Bs * P

    xp = x_prompt.reshape(Mp, D)
    xs = jnp.pad(x_sample, ((0, 0), (0, P - 1), (0, 0))).reshape(Ms, D)

    tmp = _row_tile(Mp, 1024)
    tf = 256
    tn = 512

    def first_rows(a):
        return a.reshape(Bs, P, -1)[:, 0]

    def pad_rows(a):
        return jnp.pad(a[:, None, :], ((0, 0), (0, P - 1), (0, 0))).reshape(Bs * P, -1)

    mem_flat = mem_prompt.reshape(Bp * n_mem, D)
    mem_wq_b, mem_wo_b = mem_w_q.astype(BF16), mem_w_o.astype(BF16)
    gla_w_in_b, gla_w_out_b = gla_w_in.astype(BF16), gla_w_out.astype(BF16)
    sg_w_in_b, sg_w_out_b = sg_w_in.astype(BF16), sg_w_out.astype(BF16)
    sb_w_qkv_b, sb_w_out_b = sb_w_qkv.astype(BF16), sb_w_out.astype(BF16)

    def out_proj(y, w_b, layer, res):
        return _mm(y, w_b, layer, tm=_row_tile(y.shape[0], 512), tn=w_b.shape[2], res=res)
    mem_k_new, mem_v_new = [], []
    gla_p, gla_s, pool_p, pool_s, sg_s = [], [], [], [], []
    sbk_p, sbv_p, sbk_s, sbv_s = [], [], [], []

    for i in range(depth):
        kind, j = i % 4, i // 4
        xp, xs = _ffn(xp, xs, ffn_a_norm[i], ffn_a_wi, ffn_a_wo, i, tm=tmp, tf=tf)

        if kind == 0:
            H, dk = GLA_HEADS, gla_w_gate.shape[2] // GLA_HEADS
            hk = H * dk
            hv = gla_w_out.shape[1]
            n_main = 2 * hk + 2 * hv
            w_a = jnp.pad(gla_w_in[j:j + 1, :, n_main:], ((0, 0), (0, 0), (0, LANES - GLA_RANK)))
            w_g = jnp.pad(gla_w_gate[j], ((0, LANES - GLA_RANK), (0, 0)))
            s0p = jnp.zeros((Bp, H, dk, hv // H), F32)
            outs = []
            for x, rows, c, valid, s0, tm in ((xp, T, GLA_CHUNK, GLA_CHUNK, s0p, tmp),
                                              (xs, P, P, 1, state_gla[j], Ms)):
                z = _mm(x, gla_w_in_b, j, tm=tm, tn=tn, n_cols=n_main, gain=mix_norm[i])
                a = _mm(x, w_a, 0, tm=tm, tn=LANES, gain=mix_norm[i])
                y, st = _gla(z, a, w_g, gla_b_gate[j], gla_out_norm[j], s0, rows_per_seq=rows, c=c, valid=valid)
                outs.append((out_proj(y, gla_w_out_b, j, x), st))
            (xp, st_p), (xs, st_s) = outs
            gla_p.append(st_p)
            gla_s.append(st_s)
        elif kind == 1:
            W = sg_w_out.shape[1]
            outs = []
            for x, c, nck, tm in ((xp, SG_CHUNK, 4, tmp), (xs, P, 1, Ms)):
                uv = _mm(x, sg_w_in_b, j, tm=tm, tn=tn, gain=mix_norm[i], act="gelu")
                b_full = jnp.repeat(sg_b_s[j][:, :c].T, W // SG_GROUPS, axis=1)
                w_s = jnp.pad(sg_w_s[j][:, :c, :c], ((0, 0), (0, SG_CHUNK - c), (0, SG_CHUNK - c)))
                vn, y = _sg(uv, sg_v_norm[j], w_s, b_full, c=c, n_chunks=nck)
                outs.append((out_proj(y, sg_w_out_b, j, x), vn))
            (xp, _), (xs, vn_s) = outs
            sg_s.append(first_rows(vn_s)[:, None, :])
        elif kind == 2:
            hp, xp = _pool(xp, xp, mix_norm[i], pool_w, pool_scale[j], j, rows_per_seq=T,
                           tm=_row_tile(T, 512), normalize=True)
            pool_p.append(hp.reshape(Bp, T, D)[:, T - POOL_BUF:])
            assert POOL_BUF + 1 == POOL_HALO
            hs = first_rows(_rmsnorm(xs, mix_norm[i]))
            hc = jnp.concatenate([state_pool[j], hs[:, None, :]], axis=1)
            res = jnp.pad(first_rows(xs)[:, None, :], ((0, 0), (POOL_BUF, 0), (0, 0)))
            _, out = _pool(hc.reshape(Bs * POOL_HALO, D), res.reshape(Bs * POOL_HALO, D), mix_norm[i],
                           pool_w, pool_scale[j], j, rows_per_seq=POOL_HALO, tm=POOL_HALO, normalize=False)
            xs = pad_rows(out.reshape(Bs, POOL_HALO, D)[:, POOL_HALO - 1])
            pool_s.append(hc[:, POOL_HALO - POOL_BUF:])
        else:
            hd = D // SB_HEADS
            hg = jnp.concatenate([jnp.tile(sb_q_norm[j], SB_HEADS), jnp.tile(sb_k_norm[j], SB_HEADS),
                                  jnp.ones((D,), F32)]).reshape(1, 3 * D)
            n_norm = 2 * D // tn
            qkv_p = _mm(xp, sb_w_qkv_b, j, tm=tmp, tn=tn, gain=mix_norm[i], headnorm=(hg, n_norm))
            att_p = _sb_prompt(qkv_p, sb_logit_bias[j], rows_per_seq=T, t=256)
            xp = out_proj(att_p, sb_w_out_b, j, xp)
            sbk_p.append(qkv_p[:, D:2 * D].reshape(Bp, T, SB_HEADS, hd))
            sbv_p.append(qkv_p[:, 2 * D:].reshape(Bp, T, SB_HEADS, hd))
            qkv_s = first_rows(_mm(xs, sb_w_qkv_b, j, tm=Ms, tn=tn, gain=mix_norm[i], headnorm=(hg, n_norm)))
            q_s, k_s, v_s = (qkv_s[:, n * D:(n + 1) * D].reshape(Bs, SB_HEADS, hd) for n in range(3))
            att_s = _sb_sample(q_s, sb_logit_bias[j], cache_sb_k, cache_sb_v, page_table, j)
            xs = out_proj(pad_rows(att_s.reshape(Bs, D)), sb_w_out_b, j, xs)
            sbk_s.append(k_s[:, None])
            sbv_s.append(v_s[:, None])

        hg_k = jnp.tile(mem_k_norm[i], MEM_HEADS).reshape(1, mem_w)
        mk = _mm(mem_flat, mem_w_k, i, tm=Bp * n_mem, tn=mem_w, gain=mem_in_norm[i], headnorm=(hg_k, 1))
        mv = _mm(mem_flat, mem_w_v, i, tm=Bp * n_mem, tn=mem_w, gain=mem_in_norm[i])
        mem_k_new.append(mk.reshape(Bp, n_mem, MEM_HEADS, MEM_HD))
        mem_v_new.append(mv.reshape(Bp, n_mem, MEM_HEADS, MEM_HD))
        xp = _mem_attend(xp, mem_x_norm[i], mem_wq_b, mem_q_norm[i], mk.reshape(Bp, n_mem, mem_w),
                         mv.reshape(Bp, n_mem, mem_w), mem_wo_b, i, rows_per_seq=T, tm=_row_tile(T, 512))
        xs = _mem_attend(xs, mem_x_norm[i], mem_wq_b, mem_q_norm[i], cache_mem_k[i].reshape(Bs, n_mem, mem_w),
                         cache_mem_v[i].reshape(Bs, n_mem, mem_w), mem_wo_b, i, rows_per_seq=P, tm=P)

        xp, xs = _ffn(xp, xs, ffn_b_norm[i], ffn_b_wi, ffn_b_wo, i, tm=tmp, tf=tf)

    return (xp.reshape(Bp, T, D), first_rows(xs)[:, None, :], jnp.stack(mem_k_new), jnp.stack(mem_v_new),
            jnp.stack(gla_p), jnp.stack(gla_s), jnp.stack(pool_p), jnp.stack(pool_s), jnp.stack(sg_s),
            jnp.stack(sbk_p), jnp.stack(sbv_p), jnp.stack(sbk_s), jnp.stack(sbv_s))
```

```python
import functools

import jax
import jax.numpy as jnp
from jax import lax
from jax.experimental import pallas as pl
from jax.experimental.pallas import tpu as pltpu

F32 = jnp.float32
BF16 = jnp.bfloat16

EPS = 1e-6
LANES = 128
SAMPLE_PAD = 8
VMEM_LIMIT_BYTES = 56 * 1024 * 1024

MEM_HEADS = 4
MEM_HD = 128
GLA_HEADS = 4
GLA_RANK = 16
GLA_TAU = 16.0
GLA_CHUNK = 64
SG_CHUNK = 128
SG_GROUPS = 8
POOL_WINDOWS = (2, 4, 8, 16)
POOL_BUF = max(POOL_WINDOWS) - 1
POOL_HALO = 16
SB_HEADS = 16
SB_HEADS_PER_STEP = 4
SB_PAGES_PER_STEP = 8


def _params(*sem):
    return pltpu.CompilerParams(dimension_semantics=sem, vmem_limit_bytes=VMEM_LIMIT_BYTES)


def _rms(x, g):
    ms = jnp.mean(x * x, axis=-1, keepdims=True)
    return x * lax.rsqrt(ms + EPS) * g


def _bdot(a, b):
    return jnp.dot(a.astype(BF16), b.astype(BF16), preferred_element_type=F32)


def _bdot_nt(a, b):
    return lax.dot_general(a.astype(BF16), b.astype(BF16), (((1,), (1,)), ((), ())),
                           preferred_element_type=F32)


def _softplus_core(z):
    return jnp.log(1.0 + jnp.exp(-jnp.abs(z)))


LOG2E = 1.4426950408889634


def _split2(x):
    hi = x.astype(BF16)
    return hi, (x - hi.astype(F32)).astype(BF16)


def _split3(x):
    hi = x.astype(BF16)
    r1 = x - hi.astype(F32)
    mid = r1.astype(BF16)
    lo = (r1 - mid.astype(F32)).astype(BF16)
    return hi, mid, lo


def _later_matrix(n):
    return (lax.broadcasted_iota(jnp.int32, (n, n), 0) >
            lax.broadcasted_iota(jnp.int32, (n, n), 1)).astype(BF16)


def _row(a):
    return a.reshape(1, a.shape[-1])


def _ffn_body(x_ref, xs_ref, g_ref, wg_ref, wu_ref, wo_ref, o_ref, os_ref, h_ref, hs_ref, *, n_ff):
    i = pl.program_id(0)
    j = pl.program_id(1)

    def half_step(x_ref, o_ref, h_ref):
        @pl.when(j == 0)
        def _():
            h_ref[...] = _rms(x_ref[...], g_ref[...]).astype(BF16)
            o_ref[...] = jnp.zeros_like(o_ref)

        h = h_ref[...]
        g = jnp.dot(h, wg_ref[...].astype(BF16), preferred_element_type=F32)
        u = jnp.dot(h, wu_ref[...].astype(BF16), preferred_element_type=F32)
        a = (g * jax.nn.sigmoid(g) * u).astype(BF16)
        o_ref[...] += jnp.dot(a, wo_ref[...].astype(BF16), preferred_element_type=F32)

        @pl.when(j == n_ff - 1)
        def _():
            o_ref[...] = x_ref[...] + 0.5 * o_ref[...]

    half_step(x_ref, o_ref, h_ref)

    @pl.when(i == 0)
    def _():
        half_step(xs_ref, os_ref, hs_ref)


def _ffn(x, xs, gain, wi, wo, layer, *, tm, tf):
    M, D = x.shape
    Ms = xs.shape[0]
    F = wo.shape[1]
    n_ff = F // tf
    assert M % tm == 0 and F % tf == 0
    return pl.pallas_call(
        functools.partial(_ffn_body, n_ff=n_ff),
        grid=(M // tm, n_ff),
        in_specs=[
            pl.BlockSpec((tm, D), lambda i, j: (i, 0)),
            pl.BlockSpec((Ms, D), lambda i, j: (0, 0)),
            pl.BlockSpec((1, D), lambda i, j: (0, 0)),
            pl.BlockSpec((None, D, tf), lambda i, j: (layer, 0, j)),
            pl.BlockSpec((None, D, tf), lambda i, j: (layer, 0, j + n_ff)),
            pl.BlockSpec((None, tf, D), lambda i, j: (layer, j, 0)),
        ],
        out_specs=[pl.BlockSpec((tm, D), lambda i, j: (i, 0)), pl.BlockSpec((Ms, D), lambda i, j: (0, 0))],
        out_shape=[jax.ShapeDtypeStruct((M, D), F32), jax.ShapeDtypeStruct((Ms, D), F32)],
        scratch_shapes=[pltpu.VMEM((tm, D), BF16), pltpu.VMEM((Ms, D), BF16)],
        compiler_params=_params("arbitrary", "arbitrary"),
        name="ffn",
    )(x, xs, _row(gain), wi, wi, wo)


def _mm_body(*refs, has_gain, act, has_res, n_headnorm):
    it = iter(refs)
    x_ref = next(it)
    g_ref = next(it) if has_gain else None
    w_ref = next(it)
    hg_ref = next(it) if n_headnorm else None
    r_ref = next(it) if has_res else None
    o_ref = next(it)
    h_ref = next(it)
    j = pl.program_id(1)

    @pl.when(j == 0)
    def _():
        x = x_ref[...]
        if has_gain:
            x = _rms(x, g_ref[...])
        h_ref[...] = x.astype(BF16)

    y = jnp.dot(h_ref[...], w_ref[...].astype(BF16), preferred_element_type=F32)
    if act == "gelu":
        y = jax.nn.gelu(y, approximate=True)
    if has_res:
        y = r_ref[...] + y
    o_ref[...] = y

    if n_headnorm:
        @pl.when(j < n_headnorm)
        def _():
            tn = o_ref.shape[1]
            for s in range(tn // LANES):
                sl = slice(s * LANES, (s + 1) * LANES)
                o_ref[:, sl] = _rms(y[:, sl], hg_ref[:, sl])


def _mm(x, w, layer, *, tm, tn, n_cols=None, gain=None, act=None, res=None, headnorm=None):
    M, K = x.shape
    N = n_cols if n_cols is not None else w.shape[2]
    assert M % tm == 0 and N % tn == 0
    args, specs = [x], [pl.BlockSpec((tm, K), lambda i, j: (i, 0))]
    if gain is not None:
        args.append(_row(gain))
        specs.append(pl.BlockSpec((1, K), lambda i, j: (0, 0)))
    args.append(w)
    specs.append(pl.BlockSpec((None, K, tn), lambda i, j: (layer, 0, j)))
    n_headnorm = 0
    if headnorm is not None:
        hg, n_headnorm = headnorm
        args.append(hg)
        specs.append(pl.BlockSpec((1, tn), lambda i, j: (0, j)))
    if res is not None:
        args.append(res)
        specs.append(pl.BlockSpec((tm, tn), lambda i, j: (i, j)))
    return pl.pallas_call(
        functools.partial(_mm_body, has_gain=gain is not None, act=act, has_res=res is not None,
                          n_headnorm=n_headnorm),
        grid=(M // tm, N // tn),
        in_specs=specs,
        out_specs=pl.BlockSpec((tm, tn), lambda i, j: (i, j)),
        out_shape=jax.ShapeDtypeStruct((M, N), F32),
        scratch_shapes=[pltpu.VMEM((tm, K), BF16)],
        compiler_params=_params("parallel", "arbitrary"),
        name="mm",
    )(*args)


def _memattn_body(x_ref, gx_ref, wq_ref, gq_ref, k_ref, v_ref, wo_ref, o_ref):
    x = x_ref[...]
    h = _rms(x, gx_ref[...])
    q = _bdot(h, wq_ref[...])
    scale = MEM_HD ** -0.5
    outs = []
    for hd in range(MEM_HEADS):
        sl = slice(hd * MEM_HD, (hd + 1) * MEM_HD)
        qh = _rms(q[:, sl], gq_ref[...])
        s = _bdot_nt(qh, k_ref[:, sl]) * scale
        s = s - jnp.max(s, axis=-1, keepdims=True)
        e = jnp.exp(s)
        p = e / jnp.sum(e, axis=-1, keepdims=True)
        outs.append(_bdot(p, v_ref[:, sl]))
    o = jnp.concatenate(outs, axis=-1)
    o_ref[...] = x + _bdot(o, wo_ref[...])


def _mem_attend(x, gx, wq, gq, k, v, wo, layer, *, rows_per_seq, tm):
    M, D = x.shape
    B, N, W = k.shape
    nt = rows_per_seq // tm
    assert rows_per_seq % tm == 0 and M == B * rows_per_seq
    return pl.pallas_call(
        _memattn_body,
        grid=(B, nt),
        in_specs=[
            pl.BlockSpec((tm, D), lambda b, i: (b * nt + i, 0)),
            pl.BlockSpec((1, D), lambda b, i: (0, 0)),
            pl.BlockSpec((None, D, W), lambda b, i: (layer, 0, 0)),
            pl.BlockSpec((1, MEM_HD), lambda b, i: (0, 0)),
            pl.BlockSpec((None, N, W), lambda b, i: (b, 0, 0)),
            pl.BlockSpec((None, N, W), lambda b, i: (b, 0, 0)),
            pl.BlockSpec((None, W, D), lambda b, i: (layer, 0, 0)),
        ],
        out_specs=pl.BlockSpec((tm, D), lambda b, i: (b * nt + i, 0)),
        out_shape=jax.ShapeDtypeStruct((M, D), F32),
        compiler_params=_params("parallel", "arbitrary"),
        name="mem_attend",
    )(x, _row(gx), wq, _row(gq), k, v, wo)


def _gla_body(q_ref, k_ref, v_ref, r_ref, a_ref, wg_ref, bg_ref, go_ref, s0_ref, y_ref, s_ref,
              *, c, valid, dk, dv):
    i = pl.program_id(1)

    @pl.when(i == 0)
    def _():
        s_ref[...] = s0_ref[...]

    R = LANES
    rows = lax.broadcasted_iota(jnp.int32, (R, 1), 0)
    live = rows < valid
    tri_r = (lax.broadcasted_iota(jnp.int32, (R, R), 1) <= lax.broadcasted_iota(jnp.int32, (R, R), 0))
    tri_b = tri_r.astype(BF16)
    tri_c = tri_r[:c, :]

    def pad(x):
        return jnp.concatenate([x, jnp.zeros((R - c, x.shape[1]), F32)], axis=0)

    a = pad(a_ref[...])
    heads = range(GLA_HEADS)
    kss = [slice(hd * dk, (hd + 1) * dk) for hd in heads]
    vss = [slice(hd * dv, (hd + 1) * dv) for hd in heads]
    las = [_bdot(a, wg_ref[:, ks]) + bg_ref[:, ks] for ks in kss]
    parts = []
    for la in las:
        la = (jnp.minimum(la, 0.0) - _softplus_core(la)) / GLA_TAU
        parts.append(jnp.concatenate(_split3(jnp.where(live, la, 0.0)), axis=1))
    bs = []
    for p in parts:
        c3 = jnp.dot(tri_b, p, preferred_element_type=F32)
        bs.append(c3[:, :dk] + c3[:, dk:2 * dk] + c3[:, 2 * dk:])
    ks_, vs_, qds, kds, sts = [], [], [], [], []
    for hd in heads:
        b = bs[hd]
        b_last = b[R - 1:R, :]
        k = jnp.where(live, pad(k_ref[:, kss[hd]]), 0.0)
        vs_.append(jnp.where(live, pad(v_ref[:, vss[hd]]), 0.0))
        qds.append(q_ref[:, kss[hd]] * (dk ** -0.5) * jnp.exp(b[:c, :]))
        kds.append(k * jnp.exp(-b))
        sts.append(jnp.where(rows < R - 1, k * jnp.exp(b_last - b), jnp.exp(b_last)))
    atts = [jnp.where(tri_c, _bdot_nt(qds[hd], kds[hd]), 0.0) for hd in heads]
    states = [s_ref[0, hd] for hd in heads]
    os_ = [_bdot(atts[hd], vs_[hd]) + _bdot(qds[hd], states[hd]) for hd in heads]
    for hd in heads:
        st = sts[hd].T
        s_ref[0, hd] = st[:, R - 1:R] * states[hd] + _bdot(st, vs_[hd])
    for hd in heads:
        r = r_ref[:, vss[hd]]
        y_ref[:, vss[hd]] = _rms(os_[hd], go_ref[...]) * (r * jax.nn.sigmoid(r))


def _gla(z, a, w_gate_pad, b_gate, g_out, s0, *, rows_per_seq, c, valid):
    M = z.shape[0]
    B, H, dk, dv = s0.shape
    hk, hv = H * dk, H * dv
    n = rows_per_seq // c
    assert rows_per_seq % c == 0 and c < LANES and hv == 2 * hk
    body = functools.partial(_gla_body, c=c, valid=valid, dk=dk, dv=dv)
    return pl.pallas_call(
        body,
        grid=(B, n),
        in_specs=[
            pl.BlockSpec((c, hk), lambda b, i: (b * n + i, 0)),
            pl.BlockSpec((c, hk), lambda b, i: (b * n + i, 1)),
            pl.BlockSpec((c, hv), lambda b, i: (b * n + i, 1)),
            pl.BlockSpec((c, hv), lambda b, i: (b * n + i, 2)),
            pl.BlockSpec((c, LANES), lambda b, i: (b * n + i, 0)),
            pl.BlockSpec((LANES, hk), lambda b, i: (0, 0)),
            pl.BlockSpec((1, hk), lambda b, i: (0, 0)),
            pl.BlockSpec((1, dv), lambda b, i: (0, 0)),
            pl.BlockSpec((1, H, dk, dv), lambda b, i: (b, 0, 0, 0)),
        ],
        out_specs=[
            pl.BlockSpec((c, hv), lambda b, i: (b * n + i, 0)),
            pl.BlockSpec((1, H, dk, dv), lambda b, i: (b, 0, 0, 0)),
        ],
        out_shape=[jax.ShapeDtypeStruct((M, hv), F32), jax.ShapeDtypeStruct((B, H, dk, dv), F32)],
        compiler_params=_params("parallel", "arbitrary"),
        name="gla",
    )(z, z, z, z, a, w_gate_pad, b_gate.reshape(1, hk), g_out.reshape(1, dv), s0)


def _sg_body(u_ref, v_ref, gv_ref, ws_ref, bs_ref, vn_ref, y_ref, *, c, n_chunks, gw):
    R = ws_ref.shape[1]
    tri = (lax.broadcasted_iota(jnp.int32, (R, R), 1) <= lax.broadcasted_iota(jnp.int32, (R, R), 0))
    vn = _rms(v_ref[...], gv_ref[...])
    vn_ref[...] = vn
    for g in range(SG_GROUPS):
        w = jnp.where(tri, ws_ref[g], 0.0).astype(BF16)
        gs = slice(g * gw, (g + 1) * gw)
        for t in range(n_chunks):
            rs = slice(t * c, (t + 1) * c)
            vc = vn[rs, gs]
            if c < R:
                vc = jnp.concatenate([vc, jnp.zeros((R - c, gw), F32)], axis=0)
            mixed = jnp.dot(w, vc.astype(BF16), preferred_element_type=F32)[:c, :] + bs_ref[:, gs]
            y_ref[rs, gs] = u_ref[rs, gs] * mixed


def _sg(uv, g_v, w_s, b_full, *, c, n_chunks):
    M, W2 = uv.shape
    W = W2 // 2
    tm = c * n_chunks
    assert M % tm == 0
    body = functools.partial(_sg_body, c=c, n_chunks=n_chunks, gw=W // SG_GROUPS)
    return pl.pallas_call(
        body,
        grid=(M // tm,),
        in_specs=[
            pl.BlockSpec((tm, W), lambda i: (i, 0)),
            pl.BlockSpec((tm, W), lambda i: (i, 1)),
            pl.BlockSpec((1, W), lambda i: (0, 0)),
            pl.BlockSpec(w_s.shape, lambda i: (0, 0, 0)),
            pl.BlockSpec((c, W), lambda i: (0, 0)),
        ],
        out_specs=[pl.BlockSpec((tm, W), lambda i: (i, 0)), pl.BlockSpec((tm, W), lambda i: (i, 0))],
        out_shape=[jax.ShapeDtypeStruct((M, W), F32), jax.ShapeDtypeStruct((M, W), F32)],
        compiler_params=_params("parallel"),
        name="sg",
    )(uv, uv, g_v.reshape(1, W), w_s, b_full)


def _rmsnorm_body(x_ref, g_ref, o_ref):
    o_ref[...] = _rms(x_ref[...], g_ref[...])


def _rmsnorm(x, gain):
    M, D = x.shape
    return pl.pallas_call(
        _rmsnorm_body,
        grid=(1,),
        in_specs=[pl.BlockSpec((M, D), lambda i: (0, 0)), pl.BlockSpec((1, D), lambda i: (0, 0))],
        out_specs=pl.BlockSpec((M, D), lambda i: (0, 0)),
        out_shape=jax.ShapeDtypeStruct((M, D), F32),
        name="rmsnorm",
    )(x, gain.reshape(1, D))


def _pool_body(x_ref, halo_ref, res_ref, g_ref, wp_ref, sc_ref, h_ref, o_ref, *, tm, gw, normalize):
    i = pl.program_id(1)
    h, halo = x_ref[...], halo_ref[...]
    if normalize:
        h, halo = _rms(h, g_ref[...]), _rms(halo, g_ref[...])
    h_ref[...] = h
    halo = jnp.where(i > 0, halo, 0.0)
    hc = jnp.concatenate([halo, h], axis=0)
    pos = i * tm + lax.broadcasted_iota(jnp.int32, (tm, 1), 0)
    for g, w in enumerate(POOL_WINDOWS):
        gs = slice(g * gw, (g + 1) * gw)
        s = hc[:, gs]
        span, n = 1, s.shape[0]
        while span < w:
            s = s[span:, :] + s[:n - span, :]
            n -= span
            span *= 2
        win = s[n - tm:, :]
        cnt = jnp.minimum(pos + 1, w).astype(F32)
        pooled = win / cnt - h[:, gs]
        o_ref[:, gs] = res_ref[:, gs] + _bdot(pooled, wp_ref[g]) * sc_ref[:, gs]


def _pool(x, res, gain, w_pool, scale, layer, *, rows_per_seq, tm, normalize):
    M, D = x.shape
    nt = rows_per_seq // tm
    hb = tm // POOL_HALO
    G = len(POOL_WINDOWS)
    gw = D // G
    assert rows_per_seq % tm == 0 and tm % POOL_HALO == 0
    body = functools.partial(_pool_body, tm=tm, gw=gw, normalize=normalize)
    return pl.pallas_call(
        body,
        grid=(M // rows_per_seq, nt),
        in_specs=[
            pl.BlockSpec((tm, D), lambda b, i: (b * nt + i, 0)),
            pl.BlockSpec((POOL_HALO, D), lambda b, i: (jnp.maximum((b * nt + i) * hb - 1, 0), 0)),
            pl.BlockSpec((tm, D), lambda b, i: (b * nt + i, 0)),
            pl.BlockSpec((1, D), lambda b, i: (0, 0)),
            pl.BlockSpec((None, G, gw, gw), lambda b, i: (layer, 0, 0, 0)),
            pl.BlockSpec((1, D), lambda b, i: (0, 0)),
        ],
        out_specs=[pl.BlockSpec((tm, D), lambda b, i: (b * nt + i, 0)),
                   pl.BlockSpec((tm, D), lambda b, i: (b * nt + i, 0))],
        out_shape=[jax.ShapeDtypeStruct((M, D), F32), jax.ShapeDtypeStruct((M, D), F32)],
        compiler_params=_params("parallel", "arbitrary"),
        name="pool",
    )(x, x, res, gain.reshape(1, D), w_pool, scale.reshape(1, D))


def _sb_body(bias_ref, q_ref, k_ref, v_ref, o_ref, carry_ref, *, t, nh, scale):
    hp = pl.program_id(1)
    qi = pl.program_id(2)
    later_m = _later_matrix(t)
    causal = (lax.broadcasted_iota(jnp.int32, (t, t), 1) < lax.broadcasted_iota(jnp.int32, (t, t), 0))

    heads = [slice(h * LANES, (h + 1) * LANES) for h in range(nh)]

    def tiles(starts, diagonal):
        chains = [(s, h) for s in range(len(starts)) for h in range(nh)]
        zs = [_bdot_nt(q_ref[:, heads[h]], k_ref[pl.ds(starts[s], t), heads[h]]) for s, h in chains]
        log_betas, log_1ms = [], []
        for (s, h), z in zip(chains, zs):
            z2 = z * (scale * LOG2E) + bias_ref[hp * nh + h] * LOG2E
            core = jnp.log2(1.0 + jnp.exp2(-jnp.abs(z2)))
            log_beta = jnp.minimum(z2, 0.0) - core
            log_1m = log_beta - z2
            if diagonal and s == 0:
                log_1m = jnp.where(causal, log_1m, 0.0)
            log_betas.append(log_beta)
            log_1ms.append(log_1m)
        laters = [jnp.dot(l.astype(BF16), later_m, preferred_element_type=F32) for l in log_1ms]
        ws = []
        for s, h in chains:
            n = s * nh + h
            first = diagonal and s == 0
            between = laters[n] if first else laters[n] + carry_ref[h]
            carry_ref[h] = between[:, 0:1] + log_1ms[n][:, 0:1]
            w = jnp.exp2(log_betas[n] + between)
            ws.append(jnp.where(causal, w, 0.0) if first else w)
        for h in range(nh):
            pv = _bdot(ws[h], v_ref[pl.ds(starts[0], t), heads[h]])
            for s in range(1, len(starts)):
                pv = pv + _bdot(ws[s * nh + h], v_ref[pl.ds(starts[s], t), heads[h]])
            if diagonal:
                o_ref[:, heads[h]] = pv
            else:
                o_ref[:, heads[h]] += pv

    def at(tile_index):
        return pl.multiple_of(tile_index * t, t)

    odd = qi % 2

    @pl.when(odd == 0)
    def _():
        tiles([at(qi)], True)

    @pl.when(odd == 1)
    def _():
        tiles([at(qi), at(qi - 1)], True)

    def step(it, _):
        newest = qi - odd - 1 - 2 * it
        tiles([at(newest), at(newest - 1)], False)
        return 0

    lax.fori_loop(0, (qi - odd) // 2, step, 0)


def _sb_prompt(qkv, bias, *, rows_per_seq, t):
    M, D3 = qkv.shape
    D = D3 // 3
    hd = D // SB_HEADS
    nh = SB_HEADS_PER_STEP
    B = M // rows_per_seq
    nq = rows_per_seq // t
    groups = SB_HEADS // nh
    assert hd == LANES and rows_per_seq % t == 0 and SB_HEADS % nh == 0
    body = functools.partial(_sb_body, t=t, nh=nh, scale=hd ** -0.5)
    return pl.pallas_call(
        body,
        grid=(B, groups, nq),
        in_specs=[
            pl.BlockSpec(memory_space=pltpu.SMEM),
            pl.BlockSpec((t, nh * hd), lambda b, g, i: (b * nq + i, g)),
            pl.BlockSpec((rows_per_seq, nh * hd), lambda b, g, i: (b, groups + g)),
            pl.BlockSpec((rows_per_seq, nh * hd), lambda b, g, i: (b, 2 * groups + g)),
        ],
        out_specs=pl.BlockSpec((t, nh * hd), lambda b, g, i: (b * nq + i, g)),
        out_shape=jax.ShapeDtypeStruct((M, D), F32),
        scratch_shapes=[pltpu.VMEM((nh, t, 1), F32)],
        compiler_params=_params("parallel", "parallel", "arbitrary"),
        name="sb_prompt",
    )(bias, qkv, qkv, qkv)


def _sbs_body(pt_ref, q_ref, bias_ref, pick_ref, spread_ref, *refs, scale, n_steps, G):
    k_refs, v_refs = refs[:G], refs[G:2 * G]
    o_ref, acc_ref, carry_ref = refs[2 * G:]
    j = pl.program_id(1)
    H, hd = q_ref.shape
    ps = k_refs[0].shape[0]
    flat = ps * H

    @pl.when(j == 0)
    def _():
        acc_ref[...] = jnp.zeros_like(acc_ref)
        carry_ref[...] = jnp.zeros_like(carry_ref)

    own = (lax.broadcasted_iota(jnp.int32, (H, flat), 1) % H) == lax.broadcasted_iota(jnp.int32, (H, flat), 0)
    q = q_ref[...]
    nt = (((1,), (1,)), ((), ()))
    raw = [jnp.where(own, lax.dot_general(q, k_refs[g][...].reshape(flat, hd), nt,
                                          preferred_element_type=F32), 0.0) for g in range(G)]
    raw = jnp.concatenate(raw, axis=0)
    hi, mid, lo = _split3(raw)
    pick = pick_ref[...]
    z = (jnp.dot(hi, pick, preferred_element_type=F32) + jnp.dot(mid, pick, preferred_element_type=F32)
         + jnp.dot(lo, pick, preferred_element_type=F32))
    z = z * scale + bias_ref[...]
    core = _softplus_core(z)
    log_beta = jnp.minimum(z, 0.0) - core
    log_1m = jnp.minimum(-z, 0.0) - core
    l_hi, l_lo = _split2(log_1m)
    later_m = _later_matrix(ps)
    later = jnp.dot(l_hi, later_m, preferred_element_type=F32) + jnp.dot(l_lo, later_m, preferred_element_type=F32)
    total = later[:, 0:1] + log_1m[:, 0:1]
    carry = carry_ref[...]
    betweens = []
    for g in range(G):
        rs = slice(g * H, (g + 1) * H)
        betweens.append(later[rs, :] + carry)
        carry = carry + total[rs, :]
    carry_ref[...] = carry
    w = jnp.exp(log_beta + jnp.concatenate(betweens, axis=0)).astype(BF16)
    wide = jnp.dot(w, spread_ref[...], preferred_element_type=F32)
    acc = acc_ref[...]
    for g in range(G):
        wg = jnp.where(own, wide[g * H:(g + 1) * H, :], 0.0)
        acc = acc + jnp.dot(wg, v_refs[g][...].reshape(flat, hd), preferred_element_type=F32)
    acc_ref[...] = acc

    @pl.when(j == n_steps - 1)
    def _():
        o_ref[...] = acc


def _sb_sample(q, bias, cache_k, cache_v, page_table, layer):
    B, H, hd = q.shape
    n_pages = page_table.shape[1]
    ps = cache_k.shape[2]
    G = SB_PAGES_PER_STEP
    assert n_pages % G == 0
    n_steps = n_pages // G
    flat = ps * H
    pick = (jnp.arange(flat)[:, None] // H == jnp.arange(ps)[None, :]).astype(BF16)
    spread = pick.T
    bias_t = jnp.broadcast_to(jnp.tile(bias.astype(F32), G)[:, None], (G * H, ps))

    def page_spec(g):
        return pl.BlockSpec((None, None, ps, H, hd),
                            lambda b, j, pt: (layer, pt[b, n_pages - 1 - (j * G + g)], 0, 0, 0))

    body = functools.partial(_sbs_body, scale=hd ** -0.5, n_steps=n_steps, G=G)
    grid_spec = pltpu.PrefetchScalarGridSpec(
        num_scalar_prefetch=1,
        grid=(B, n_steps),
        in_specs=[
            pl.BlockSpec((None, H, hd), lambda b, j, pt: (b, 0, 0)),
            pl.BlockSpec((G * H, ps), lambda b, j, pt: (0, 0)),
            pl.BlockSpec((flat, ps), lambda b, j, pt: (0, 0)),
            pl.BlockSpec((ps, flat), lambda b, j, pt: (0, 0)),
        ] + [page_spec(g) for g in range(G)] + [page_spec(g) for g in range(G)],
        out_specs=pl.BlockSpec((None, H, hd), lambda b, j, pt: (b, 0, 0)),
        scratch_shapes=[pltpu.VMEM((H, hd), F32), pltpu.VMEM((H, 1), F32)],
    )
    return pl.pallas_call(
        body,
        grid_spec=grid_spec,
        out_shape=jax.ShapeDtypeStruct((B, H, hd), F32),
        compiler_params=_params("parallel", "arbitrary"),
        name="sb_sample",
    )(page_table, q, bias_t, pick, spread, *([cache_k] * G), *([cache_v] * G))


def _row_tile(m, want):
    t = min(m, want)
    while m % t:
        t //= 2
    return t


def kernel(x_prompt, x_sample, cache_mem_k, cache_mem_v, state_gla, state_pool, cache_sb_k, cache_sb_v, page_table, mem_prompt, ffn_a_norm, ffn_a_wi, ffn_a_wo, mix_norm, gla_w_in, gla_w_gate, gla_b_gate, gla_out_norm, gla_w_out, sg_w_in, sg_v_norm, sg_w_s, sg_b_s, sg_w_out, pool_w, pool_scale, sb_w_qkv, sb_q_norm, sb_k_norm, sb_logit_bias, sb_w_out, mem_x_norm, mem_in_norm, mem_w_q, mem_q_norm, mem_w_k, mem_k_norm, mem_w_v, mem_w_o, ffn_b_norm, ffn_b_wi, ffn_b_wo):
    Bp, T, D = x_prompt.shape
    Bs = x_sample.shape[0]
    depth = ffn_a_norm.shape[0]
    n_mem = mem_prompt.shape[1]
    mem_w = MEM_HEADS * MEM_HD
    P = SAMPLE_PAD
    Mp, Ms = Bp * T, Bs * P

    xp = x_prompt.reshape(Mp, D)
    xs = jnp.pad(x_sample, ((0, 0), (0, P - 1), (0, 0))).reshape(Ms, D)

    tmp = _row_tile(Mp, 1024)
    tf = 256
    tn = 1024

    def first_rows(a):
        return a.reshape(Bs, P, -1)[:, 0]

    def pad_rows(a):
        return jnp.pad(a[:, None, :], ((0, 0), (0, P - 1), (0, 0))).reshape(Bs * P, -1)

    mem_flat = mem_prompt.reshape(Bp * n_mem, D)
    mem_wq_b, mem_wo_b = mem_w_q.astype(BF16), mem_w_o.astype(BF16)
    gla_w_in_b, gla_w_out_b = gla_w_in.astype(BF16), gla_w_out.astype(BF16)
    sg_w_in_b, sg_w_out_b = sg_w_in.astype(BF16), sg_w_out.astype(BF16)
    sb_w_qkv_b, sb_w_out_b = sb_w_qkv.astype(BF16), sb_w_out.astype(BF16)

    def out_proj(y, w_b, layer, res):
        return _mm(y, w_b, layer, tm=_row_tile(y.shape[0], 512), tn=w_b.shape[2], res=res)
    mem_k_new, mem_v_new = [], []
    gla_p, gla_s, pool_p, pool_s, sg_s = [], [], [], [], []
    sbk_p, sbv_p, sbk_s, sbv_s = [], [], [], []

    for i in range(depth):
        kind, j = i % 4, i // 4
        xp, xs = _ffn(xp, xs, ffn_a_norm[i], ffn_a_wi, ffn_a_wo, i, tm=tmp, tf=tf)

        if kind == 0:
            H, dk = GLA_HEADS, gla_w_gate.shape[2] // GLA_HEADS
            hk = H * dk
            hv = gla_w_out.shape[1]
            n_main = 2 * hk + 2 * hv
            w_a = jnp.pad(gla_w_in[j:j + 1, :, n_main:], ((0, 0), (0, 0), (0, LANES - GLA_RANK)))
            w_g = jnp.pad(gla_w_gate[j], ((0, LANES - GLA_RANK), (0, 0)))
            s0p = jnp.zeros((Bp, H, dk, hv // H), F32)
            outs = []
            for x, rows, c, valid, s0, tm in ((xp, T, GLA_CHUNK, GLA_CHUNK, s0p, tmp),
                                              (xs, P, P, 1, state_gla[j], Ms)):
                z = _mm(x, gla_w_in_b, j, tm=tm, tn=tn, n_cols=n_main, gain=mix_norm[i])
                a = _mm(x, w_a, 0, tm=tm, tn=LANES, gain=mix_norm[i])
                y, st = _gla(z, a, w_g, gla_b_gate[j], gla_out_norm[j], s0, rows_per_seq=rows, c=c, valid=valid)
                outs.append((out_proj(y, gla_w_out_b, j, x), st))
            (xp, st_p), (xs, st_s) = outs
            gla_p.append(st_p)
            gla_s.append(st_s)
        elif kind == 1:
            W = sg_w_out.shape[1]
            outs = []
            for x, c, nck, tm in ((xp, SG_CHUNK, 4, tmp), (xs, P, 1, Ms)):
                uv = _mm(x, sg_w_in_b, j, tm=tm, tn=tn, gain=mix_norm[i], act="gelu")
                b_full = jnp.repeat(sg_b_s[j][:, :c].T, W // SG_GROUPS, axis=1)
                w_s = jnp.pad(sg_w_s[j][:, :c, :c], ((0, 0), (0, SG_CHUNK - c), (0, SG_CHUNK - c)))
                vn, y = _sg(uv, sg_v_norm[j], w_s, b_full, c=c, n_chunks=nck)
                outs.append((out_proj(y, sg_w_out_b, j, x), vn))
            (xp, _), (xs, vn_s) = outs
            sg_s.append(first_rows(vn_s)[:, None, :])
        elif kind == 2:
            hp, xp = _pool(xp, xp, mix_norm[i], pool_w, pool_scale[j], j, rows_per_seq=T,
                           tm=_row_tile(T, 512), normalize=True)
            pool_p.append(hp.reshape(Bp, T, D)[:, T - POOL_BUF:])
            assert POOL_BUF + 1 == POOL_HALO
            hs = first_rows(_rmsnorm(xs, mix_norm[i]))
            hc = jnp.concatenate([state_pool[j], hs[:, None, :]], axis=1)
            res = jnp.pad(first_rows(xs)[:, None, :], ((0, 0), (POOL_BUF, 0), (0, 0)))
            _, out = _pool(hc.reshape(Bs * POOL_HALO, D), res.reshape(Bs * POOL_HALO, D), mix_norm[i],
                           pool_w, pool_scale[j], j, rows_per_seq=POOL_HALO, tm=POOL_HALO, normalize=False)
            xs = pad_rows(out.reshape(Bs, POOL_HALO, D)[:, POOL_HALO - 1])
            pool_s.append(hc[:, POOL_HALO - POOL_BUF:])
        else:
            hd = D // SB_HEADS
            hg = jnp.concatenate([jnp.tile(sb_q_norm[j], SB_HEADS), jnp.tile(sb_k_norm[j], SB_HEADS),
                                  jnp.ones((D,), F32)]).reshape(1, 3 * D)
            n_norm = 2 * D // tn
            qkv_p = _mm(xp, sb_w_qkv_b, j, tm=tmp, tn=tn, gain=mix_norm[i], headnorm=(hg, n_norm))
            att_p = _sb_prompt(qkv_p, sb_logit_bias[j], rows_per_seq=T, t=256)
            xp = out_proj(att_p, sb_w_out_b, j, xp)
            sbk_p.append(qkv_p[:, D:2 * D].reshape(Bp, T, SB_HEADS, hd))
            sbv_p.append(qkv_p[:, 2 * D:].reshape(Bp, T, SB_HEADS, hd))
            qkv_s = first_rows(_mm(xs, sb_w_qkv_b, j, tm=Ms, tn=tn, gain=mix_norm[i], headnorm=(hg, n_norm)))
            q_s, k_s, v_s = (qkv_s[:, n * D:(n + 1) * D].reshape(Bs, SB_HEADS, hd) for n in range(3))
            att_s = _sb_sample(q_s, sb_logit_bias[j], cache_sb_k, cache_sb_v, page_table, j)
            xs = out_proj(pad_rows(att_s.reshape(Bs, D)), sb_w_out_b, j, xs)
            sbk_s.append(k_s[:, None])
            sbv_s.append(v_s[:, None])

        hg_k = jnp.tile(mem_k_norm[i], MEM_HEADS).reshape(1, mem_w)
        mk = _mm(mem_flat, mem_w_k, i, tm=Bp * n_mem, tn=mem_w, gain=mem_in_norm[i], headnorm=(hg_k, 1))
        mv = _mm(mem_flat, mem_w_v, i, tm=Bp * n_mem, tn=mem_w, gain=mem_in_norm[i])
        mem_k_new.append(mk.reshape(Bp, n_mem, MEM_HEADS, MEM_HD))
        mem_v_new.append(mv.reshape(Bp, n_mem, MEM_HEADS, MEM_HD))
        xp = _mem_attend(xp, mem_x_norm[i], mem_wq_b, mem_q_norm[i], mk.reshape(Bp, n_mem, mem_w),
                         mv.reshape(Bp, n_mem, mem_w), mem_wo_b, i, rows_per_seq=T, tm=_row_tile(T, 512))
        xs = _mem_attend(xs, mem_x_norm[i], mem_wq_b, mem_q_norm[i], cache_mem_k[i].reshape(Bs, n_mem, mem_w),
                         cache_mem_v[i].reshape(Bs, n_mem, mem_w), mem_wo_b, i, rows_per_seq=P, tm=P)

        xp, xs = _ffn(xp, xs, ffn_b_norm[i], ffn_b_wi, ffn_b_wo, i, tm=tmp, tf=tf)

    return (xp.reshape(Bp, T, D), first_rows(xs)[:, None, :], jnp.stack(mem_k_new), jnp.stack(mem_v_new),
            jnp.stack(gla_p), jnp.stack(gla_s), jnp.stack(pool_p), jnp.stack(pool_s), jnp.stack(sg_s),
            jnp.stack(sbk_p), jnp.stack(sbv_p), jnp.stack(sbk_s), jnp.stack(sbv_s))
```

```python
import functools

import jax
import jax.numpy as jnp
from jax import lax
from jax.experimental import pallas as pl
from jax.experimental.pallas import tpu as pltpu

F32 = jnp.float32
BF16 = jnp.bfloat16

EPS = 1e-6
LANES = 128
SAMPLE_PAD = 8
VMEM_LIMIT_BYTES = 56 * 1024 * 1024

MEM_HEADS = 4
MEM_HD = 128
GLA_HEADS = 4
GLA_RANK = 16
GLA_TAU = 16.0
GLA_CHUNK = 64
SG_CHUNK = 128
SG_GROUPS = 8
POOL_WINDOWS = (2, 4, 8, 16)
POOL_BUF = max(POOL_WINDOWS) - 1
POOL_HALO = 16
SB_HEADS = 16
SB_HEADS_PER_STEP = 4
SB_PAGES_PER_STEP = 8


def _params(*sem):
    return pltpu.CompilerParams(dimension_semantics=sem, vmem_limit_bytes=VMEM_LIMIT_BYTES)


def _rms(x, g):
    ms = jnp.mean(x * x, axis=-1, keepdims=True)
    return x * lax.rsqrt(ms + EPS) * g


def _bdot(a, b):
    return jnp.dot(a.astype(BF16), b.astype(BF16), preferred_element_type=F32)


def _bdot_nt(a, b):
    return lax.dot_general(a.astype(BF16), b.astype(BF16), (((1,), (1,)), ((), ())),
                           preferred_element_type=F32)


def _softplus_core(z):
    return jnp.log(1.0 + jnp.exp(-jnp.abs(z)))


LOG2E = 1.4426950408889634


def _split2(x):
    hi = x.astype(BF16)
    return hi, (x - hi.astype(F32)).astype(BF16)


def _split3(x):
    hi = x.astype(BF16)
    r1 = x - hi.astype(F32)
    mid = r1.astype(BF16)
    lo = (r1 - mid.astype(F32)).astype(BF16)
    return hi, mid, lo


def _later_matrix(n):
    return (lax.broadcasted_iota(jnp.int32, (n, n), 0) >
            lax.broadcasted_iota(jnp.int32, (n, n), 1)).astype(BF16)


def _row(a):
    return a.reshape(1, a.shape[-1])


def _ffn_body(x_ref, xs_ref, g_ref, wg_ref, wu_ref, wo_ref, o_ref, os_ref, h_ref, *, n_ff):
    i = pl.program_id(0)
    j = pl.program_id(1)
    tm = x_ref.shape[0]
    ms = xs_ref.shape[0]

    def half_step(with_sample):
        @pl.when(j == 0)
        def _():
            h_ref[:tm, :] = _rms(x_ref[...], g_ref[...]).astype(BF16)
            o_ref[...] = jnp.zeros_like(o_ref)
            if with_sample:
                h_ref[tm:, :] = _rms(xs_ref[...], g_ref[...]).astype(BF16)
                os_ref[...] = jnp.zeros_like(os_ref)

        h = h_ref[...] if with_sample else h_ref[:tm, :]
        g = jnp.dot(h, wg_ref[...].astype(BF16), preferred_element_type=F32)
        u = jnp.dot(h, wu_ref[...].astype(BF16), preferred_element_type=F32)
        a = (g * jax.nn.sigmoid(g) * u).astype(BF16)
        down = jnp.dot(a, wo_ref[...].astype(BF16), preferred_element_type=F32)
        o_ref[...] += down[:tm, :]
        if with_sample:
            os_ref[...] += down[tm:, :]

        @pl.when(j == n_ff - 1)
        def _():
            o_ref[...] = x_ref[...] + 0.5 * o_ref[...]
            if with_sample:
                os_ref[...] = xs_ref[...] + 0.5 * os_ref[...]

    @pl.when(i == 0)
    def _():
        half_step(True)

    @pl.when(i > 0)
    def _():
        half_step(False)


def _ffn(x, xs, gain, wi, wo, layer, *, tm, tf):
    M, D = x.shape
    Ms = xs.shape[0]
    F = wo.shape[1]
    n_ff = F // tf
    assert M % tm == 0 and F % tf == 0
    return pl.pallas_call(
        functools.partial(_ffn_body, n_ff=n_ff),
        grid=(M // tm, n_ff),
        in_specs=[
            pl.BlockSpec((tm, D), lambda i, j: (i, 0)),
            pl.BlockSpec((Ms, D), lambda i, j: (0, 0)),
            pl.BlockSpec((1, D), lambda i, j: (0, 0)),
            pl.BlockSpec((None, D, tf), lambda i, j: (layer, 0, j)),
            pl.BlockSpec((None, D, tf), lambda i, j: (layer, 0, j + n_ff)),
            pl.BlockSpec((None, tf, D), lambda i, j: (layer, j, 0)),
        ],
        out_specs=[pl.BlockSpec((tm, D), lambda i, j: (i, 0)), pl.BlockSpec((Ms, D), lambda i, j: (0, 0))],
        out_shape=[jax.ShapeDtypeStruct((M, D), F32), jax.ShapeDtypeStruct((Ms, D), F32)],
        scratch_shapes=[pltpu.VMEM((tm + Ms, D), BF16)],
        compiler_params=_params("arbitrary", "arbitrary"),
        name="ffn",
    )(x, xs, _row(gain), wi, wi, wo)


def _mm_body(*refs, has_gain, act, has_res, n_headnorm):
    it = iter(refs)
    x_ref = next(it)
    g_ref = next(it) if has_gain else None
    w_ref = next(it)
    hg_ref = next(it) if n_headnorm else None
    r_ref = next(it) if has_res else None
    o_ref = next(it)
    h_ref = next(it)
    j = pl.program_id(1)

    @pl.when(j == 0)
    def _():
        x = x_ref[...]
        if has_gain:
            x = _rms(x, g_ref[...])
        h_ref[...] = x.astype(BF16)

    y = jnp.dot(h_ref[...], w_ref[...].astype(BF16), preferred_element_type=F32)
    if act == "gelu":
        y = jax.nn.gelu(y, approximate=True)
    if has_res:
        y = r_ref[...] + y
    o_ref[...] = y

    if n_headnorm:
        @pl.when(j < n_headnorm)
        def _():
            tn = o_ref.shape[1]
            for s in range(tn // LANES):
                sl = slice(s * LANES, (s + 1) * LANES)
                o_ref[:, sl] = _rms(y[:, sl], hg_ref[:, sl])


def _mm(x, w, layer, *, tm, tn, n_cols=None, gain=None, act=None, res=None, headnorm=None):
    M, K = x.shape
    N = n_cols if n_cols is not None else w.shape[2]
    assert M % tm == 0 and N % tn == 0
    args, specs = [x], [pl.BlockSpec((tm, K), lambda i, j: (i, 0))]
    if gain is not None:
        args.append(_row(gain))
        specs.append(pl.BlockSpec((1, K), lambda i, j: (0, 0)))
    args.append(w)
    specs.append(pl.BlockSpec((None, K, tn), lambda i, j: (layer, 0, j)))
    n_headnorm = 0
    if headnorm is not None:
        hg, n_headnorm = headnorm
        args.append(hg)
        specs.append(pl.BlockSpec((1, tn), lambda i, j: (0, j)))
    if res is not None:
        args.append(res)
        specs.append(pl.BlockSpec((tm, tn), lambda i, j: (i, j)))
    return pl.pallas_call(
        functools.partial(_mm_body, has_gain=gain is not None, act=act, has_res=res is not None,
                          n_headnorm=n_headnorm),
        grid=(M // tm, N // tn),
        in_specs=specs,
        out_specs=pl.BlockSpec((tm, tn), lambda i, j: (i, j)),
        out_shape=jax.ShapeDtypeStruct((M, N), F32),
        scratch_shapes=[pltpu.VMEM((tm, K), BF16)],
        compiler_params=_params("parallel", "arbitrary"),
        name="mm",
    )(*args)


def _memattn_body(x_ref, gx_ref, wq_ref, gq_ref, k_ref, v_ref, wo_ref, o_ref):
    x = x_ref[...]
    h = _rms(x, gx_ref[...])
    q = _bdot(h, wq_ref[...])
    scale = MEM_HD ** -0.5
    outs = []
    for hd in range(MEM_HEADS):
        sl = slice(hd * MEM_HD, (hd + 1) * MEM_HD)
        qh = _rms(q[:, sl], gq_ref[...])
        s = _bdot_nt(qh, k_ref[:, sl]) * scale
        s = s - jnp.max(s, axis=-1, keepdims=True)
        e = jnp.exp(s)
        p = e / jnp.sum(e, axis=-1, keepdims=True)
        outs.append(_bdot(p, v_ref[:, sl]))
    o = jnp.concatenate(outs, axis=-1)
    o_ref[...] = x + _bdot(o, wo_ref[...])


def _mem_attend(x, gx, wq, gq, k, v, wo, layer, *, rows_per_seq, tm):
    M, D = x.shape
    B, N, W = k.shape
    nt = rows_per_seq // tm
    assert rows_per_seq % tm == 0 and M == B * rows_per_seq
    return pl.pallas_call(
        _memattn_body,
        grid=(B, nt),
        in_specs=[
            pl.BlockSpec((tm, D), lambda b, i: (b * nt + i, 0)),
            pl.BlockSpec((1, D), lambda b, i: (0, 0)),
            pl.BlockSpec((None, D, W), lambda b, i: (layer, 0, 0)),
            pl.BlockSpec((1, MEM_HD), lambda b, i: (0, 0)),
            pl.BlockSpec((None, N, W), lambda b, i: (b, 0, 0)),
            pl.BlockSpec((None, N, W), lambda b, i: (b, 0, 0)),
            pl.BlockSpec((None, W, D), lambda b, i: (layer, 0, 0)),
        ],
        out_specs=pl.BlockSpec((tm, D), lambda b, i: (b * nt + i, 0)),
        out_shape=jax.ShapeDtypeStruct((M, D), F32),
        compiler_params=_params("parallel", "arbitrary"),
        name="mem_attend",
    )(x, _row(gx), wq, _row(gq), k, v, wo)


def _gla_body(q_ref, k_ref, v_ref, r_ref, a_ref, wg_ref, bg_ref, go_ref, s0_ref, y_ref, s_ref,
              *, c, valid, dk, dv):
    i = pl.program_id(1)

    @pl.when(i == 0)
    def _():
        s_ref[...] = s0_ref[...]

    R = LANES
    rows = lax.broadcasted_iota(jnp.int32, (R, 1), 0)
    live = rows < valid
    tri_r = (lax.broadcasted_iota(jnp.int32, (R, R), 1) <= lax.broadcasted_iota(jnp.int32, (R, R), 0))
    tri_b = tri_r.astype(BF16)
    tri_c = tri_r[:c, :]

    def pad(x):
        return jnp.concatenate([x, jnp.zeros((R - c, x.shape[1]), F32)], axis=0)

    a = pad(a_ref[...])
    heads = range(GLA_HEADS)
    kss = [slice(hd * dk, (hd + 1) * dk) for hd in heads]
    vss = [slice(hd * dv, (hd + 1) * dv) for hd in heads]
    las = [_bdot(a, wg_ref[:, ks]) + bg_ref[:, ks] for ks in kss]
    parts = []
    for la in las:
        la = (jnp.minimum(la, 0.0) - _softplus_core(la)) / GLA_TAU
        parts.append(jnp.concatenate(_split3(jnp.where(live, la, 0.0)), axis=1))
    bs = []
    for p in parts:
        c3 = jnp.dot(tri_b, p, preferred_element_type=F32)
        bs.append(c3[:, :dk] + c3[:, dk:2 * dk] + c3[:, 2 * dk:])
    ks_, vs_, qds, kds, sts = [], [], [], [], []
    for hd in heads:
        b = bs[hd]
        b_last = b[R - 1:R, :]
        k = jnp.where(live, pad(k_ref[:, kss[hd]]), 0.0)
        vs_.append(jnp.where(live, pad(v_ref[:, vss[hd]]), 0.0))
        qds.append(q_ref[:, kss[hd]] * (dk ** -0.5) * jnp.exp(b[:c, :]))
        kds.append(k * jnp.exp(-b))
        sts.append(jnp.where(rows < R - 1, k * jnp.exp(b_last - b), jnp.exp(b_last)))
    atts = [jnp.where(tri_c, _bdot_nt(qds[hd], kds[hd]), 0.0) for hd in heads]
    states = [s_ref[0, hd] for hd in heads]
    os_ = [_bdot(atts[hd], vs_[hd]) + _bdot(qds[hd], states[hd]) for hd in heads]
    for hd in heads:
        st = sts[hd].T
        s_ref[0, hd] = st[:, R - 1:R] * states[hd] + _bdot(st, vs_[hd])
    for hd in heads:
        r = r_ref[:, vss[hd]]
        y_ref[:, vss[hd]] = _rms(os_[hd], go_ref[...]) * (r * jax.nn.sigmoid(r))


def _gla(z, a, w_gate_pad, b_gate, g_out, s0, *, rows_per_seq, c, valid):
    M = z.shape[0]
    B, H, dk, dv = s0.shape
    hk, hv = H * dk, H * dv
    n = rows_per_seq // c
    assert rows_per_seq % c == 0 and c < LANES and hv == 2 * hk
    body = functools.partial(_gla_body, c=c, valid=valid, dk=dk, dv=dv)
    return pl.pallas_call(
        body,
        grid=(B, n),
        in_specs=[
            pl.BlockSpec((c, hk), lambda b, i: (b * n + i, 0)),
            pl.BlockSpec((c, hk), lambda b, i: (b * n + i, 1)),
            pl.BlockSpec((c, hv), lambda b, i: (b * n + i, 1)),
            pl.BlockSpec((c, hv), lambda b, i: (b * n + i, 2)),
            pl.BlockSpec((c, LANES), lambda b, i: (b * n + i, 0)),
            pl.BlockSpec((LANES, hk), lambda b, i: (0, 0)),
            pl.BlockSpec((1, hk), lambda b, i: (0, 0)),
            pl.BlockSpec((1, dv), lambda b, i: (0, 0)),
            pl.BlockSpec((1, H, dk, dv), lambda b, i: (b, 0, 0, 0)),
        ],
        out_specs=[
            pl.BlockSpec((c, hv), lambda b, i: (b * n + i, 0)),
            pl.BlockSpec((1, H, dk, dv), lambda b, i: (b, 0, 0, 0)),
        ],
        out_shape=[jax.ShapeDtypeStruct((M, hv), F32), jax.ShapeDtypeStruct((B, H, dk, dv), F32)],
        compiler_params=_params("parallel", "arbitrary"),
        name="gla",
    )(z, z, z, z, a, w_gate_pad, b_gate.reshape(1, hk), g_out.reshape(1, dv), s0)


def _sg_body(u_ref, v_ref, gv_ref, ws_ref, bs_ref, *out_refs, c, n_chunks, gw):
    y_ref = out_refs[-1]
    R = ws_ref.shape[1]
    tri = (lax.broadcasted_iota(jnp.int32, (R, R), 1) <= lax.broadcasted_iota(jnp.int32, (R, R), 0))
    vn = _rms(v_ref[...], gv_ref[...])
    if len(out_refs) == 2:
        out_refs[0][...] = vn
    for g in range(SG_GROUPS):
        w = jnp.where(tri, ws_ref[g], 0.0).astype(BF16)
        gs = slice(g * gw, (g + 1) * gw)
        for t in range(n_chunks):
            rs = slice(t * c, (t + 1) * c)
            vc = vn[rs, gs]
            if c < R:
                vc = jnp.concatenate([vc, jnp.zeros((R - c, gw), F32)], axis=0)
            mixed = jnp.dot(w, vc.astype(BF16), preferred_element_type=F32)[:c, :] + bs_ref[:, gs]
            y_ref[rs, gs] = u_ref[rs, gs] * mixed


def _sg(uv, g_v, w_s, b_full, *, c, n_chunks, emit_vn):
    M, W2 = uv.shape
    W = W2 // 2
    tm = c * n_chunks
    assert M % tm == 0
    body = functools.partial(_sg_body, c=c, n_chunks=n_chunks, gw=W // SG_GROUPS)
    n_out = 2 if emit_vn else 1
    return pl.pallas_call(
        body,
        grid=(M // tm,),
        in_specs=[
            pl.BlockSpec((tm, W), lambda i: (i, 0)),
            pl.BlockSpec((tm, W), lambda i: (i, 1)),
            pl.BlockSpec((1, W), lambda i: (0, 0)),
            pl.BlockSpec(w_s.shape, lambda i: (0, 0, 0)),
            pl.BlockSpec((c, W), lambda i: (0, 0)),
        ],
        out_specs=[pl.BlockSpec((tm, W), lambda i: (i, 0))] * n_out,
        out_shape=[jax.ShapeDtypeStruct((M, W), F32)] * n_out,
        compiler_params=_params("parallel"),
        name="sg",
    )(uv, uv, g_v.reshape(1, W), w_s, b_full)


def _rmsnorm_body(x_ref, g_ref, o_ref):
    o_ref[...] = _rms(x_ref[...], g_ref[...])


def _rmsnorm(x, gain):
    M, D = x.shape
    return pl.pallas_call(
        _rmsnorm_body,
        grid=(1,),
        in_specs=[pl.BlockSpec((M, D), lambda i: (0, 0)), pl.BlockSpec((1, D), lambda i: (0, 0))],
        out_specs=pl.BlockSpec((M, D), lambda i: (0, 0)),
        out_shape=jax.ShapeDtypeStruct((M, D), F32),
        name="rmsnorm",
    )(x, gain.reshape(1, D))


def _pool_body(*refs, tm, gw, normalize, has_res):
    if has_res:
        x_ref, halo_ref, res_ref, g_ref, wp_ref, sc_ref, h_ref, o_ref = refs
    else:
        x_ref, halo_ref, g_ref, wp_ref, sc_ref, h_ref, o_ref = refs
        res_ref = x_ref
    i = pl.program_id(1)
    h, halo = x_ref[...], halo_ref[...]
    if normalize:
        h, halo = _rms(h, g_ref[...]), _rms(halo, g_ref[...])
    h_ref[...] = h
    halo = jnp.where(i > 0, halo, 0.0)
    hc = jnp.concatenate([halo, h], axis=0)
    pos = i * tm + lax.broadcasted_iota(jnp.int32, (tm, 1), 0)
    for g, w in enumerate(POOL_WINDOWS):
        gs = slice(g * gw, (g + 1) * gw)
        s = hc[:, gs]
        span, n = 1, s.shape[0]
        while span < w:
            s = s[span:, :] + s[:n - span, :]
            n -= span
            span *= 2
        win = s[n - tm:, :]
        cnt = jnp.minimum(pos + 1, w).astype(F32)
        pooled = win / cnt - h[:, gs]
        o_ref[:, gs] = res_ref[:, gs] + _bdot(pooled, wp_ref[g]) * sc_ref[:, gs]


def _pool(x, res, gain, w_pool, scale, layer, *, rows_per_seq, tm, normalize):
    M, D = x.shape
    B = M // rows_per_seq
    nt = rows_per_seq // tm
    hb = tm // POOL_HALO
    G = len(POOL_WINDOWS)
    gw = D // G
    assert rows_per_seq % tm == 0 and tm % POOL_HALO == 0
    body = functools.partial(_pool_body, tm=tm, gw=gw, normalize=normalize, has_res=res is not None)
    row_spec = pl.BlockSpec((tm, D), lambda b, i: (b * nt + i, 0))
    args = [x, x] + ([res] if res is not None else []) + [gain.reshape(1, D), w_pool, scale.reshape(1, D)]
    specs = ([row_spec, pl.BlockSpec((POOL_HALO, D), lambda b, i: (jnp.maximum((b * nt + i) * hb - 1, 0), 0))]
             + ([row_spec] if res is not None else [])
             + [pl.BlockSpec((1, D), lambda b, i: (0, 0)),
                pl.BlockSpec((None, G, gw, gw), lambda b, i: (layer, 0, 0, 0)),
                pl.BlockSpec((1, D), lambda b, i: (0, 0))])
    return pl.pallas_call(
        body,
        grid=(B, nt),
        in_specs=specs,
        out_specs=[pl.BlockSpec((tm, D), lambda b, i: (b, 0)), row_spec],
        out_shape=[jax.ShapeDtypeStruct((B * tm, D), F32), jax.ShapeDtypeStruct((M, D), F32)],
        compiler_params=_params("arbitrary", "arbitrary"),
        name="pool",
    )(*args)


def _sb_body(bias_ref, q_ref, k_ref, v_ref, o_ref, carry_ref, *, t, nh, scale):
    hp = pl.program_id(1)
    qi = pl.program_id(2)
    later_m = _later_matrix(t)
    causal = (lax.broadcasted_iota(jnp.int32, (t, t), 1) < lax.broadcasted_iota(jnp.int32, (t, t), 0))

    heads = [slice(h * LANES, (h + 1) * LANES) for h in range(nh)]

    def tiles(starts, diagonal):
        chains = [(s, h) for s in range(len(starts)) for h in range(nh)]
        zs = [_bdot_nt(q_ref[:, heads[h]], k_ref[pl.ds(starts[s], t), heads[h]]) for s, h in chains]
        log_betas, log_1ms = [], []
        for (s, h), z in zip(chains, zs):
            z2 = z * (scale * LOG2E) + bias_ref[hp * nh + h] * LOG2E
            core = jnp.log2(1.0 + jnp.exp2(-jnp.abs(z2)))
            log_beta = jnp.minimum(z2, 0.0) - core
            log_1m = log_beta - z2
            if diagonal and s == 0:
                log_1m = jnp.where(causal, log_1m, 0.0)
            log_betas.append(log_beta)
            log_1ms.append(log_1m)
        laters = [jnp.dot(l.astype(BF16), later_m, preferred_element_type=F32) for l in log_1ms]
        ws = []
        for s, h in chains:
            n = s * nh + h
            first = diagonal and s == 0
            between = laters[n] if first else laters[n] + carry_ref[h]
            carry_ref[h] = between[:, 0:1] + log_1ms[n][:, 0:1]
            w = jnp.exp2(log_betas[n] + between)
            ws.append(jnp.where(causal, w, 0.0) if first else w)
        for h in range(nh):
            pv = _bdot(ws[h], v_ref[pl.ds(starts[0], t), heads[h]])
            for s in range(1, len(starts)):
                pv = pv + _bdot(ws[s * nh + h], v_ref[pl.ds(starts[s], t), heads[h]])
            if diagonal:
                o_ref[:, heads[h]] = pv
            else:
                o_ref[:, heads[h]] += pv

    def at(tile_index):
        return pl.multiple_of(tile_index * t, t)

    odd = qi % 2

    @pl.when(odd == 0)
    def _():
        tiles([at(qi)], True)

    @pl.when(odd == 1)
    def _():
        tiles([at(qi), at(qi - 1)], True)

    def step(it, _):
        newest = qi - odd - 1 - 2 * it
        tiles([at(newest), at(newest - 1)], False)
        return 0

    lax.fori_loop(0, (qi - odd) // 2, step, 0)


def _sb_prompt(qkv, bias, *, rows_per_seq, t):
    M, D3 = qkv.shape
    D = D3 // 3
    hd = D // SB_HEADS
    nh = SB_HEADS_PER_STEP
    B = M // rows_per_seq
    nq = rows_per_seq // t
    groups = SB_HEADS // nh
    assert hd == LANES and rows_per_seq % t == 0 and SB_HEADS % nh == 0
    body = functools.partial(_sb_body, t=t, nh=nh, scale=hd ** -0.5)
    return pl.pallas_call(
        body,
        grid=(B, groups, nq),
        in_specs=[
            pl.BlockSpec(memory_space=pltpu.SMEM),
            pl.BlockSpec((t, nh * hd), lambda b, g, i: (b * nq + i, g)),
            pl.BlockSpec((rows_per_seq, nh * hd), lambda b, g, i: (b, groups + g)),
            pl.BlockSpec((rows_per_seq, nh * hd), lambda b, g, i: (b, 2 * groups + g)),
        ],
        out_specs=pl.BlockSpec((t, nh * hd), lambda b, g, i: (b * nq + i, g)),
        out_shape=jax.ShapeDtypeStruct((M, D), F32),
        scratch_shapes=[pltpu.VMEM((nh, t, 1), F32)],
        compiler_params=_params("parallel", "parallel", "arbitrary"),
        name="sb_prompt",
    )(bias, qkv, qkv, qkv)


def _sbs_body(pt_ref, q_ref, bias_ref, pick_ref, spread_ref, *refs, scale, n_steps, G):
    k_refs, v_refs = refs[:G], refs[G:2 * G]
    o_ref, acc_ref, carry_ref = refs[2 * G:]
    j = pl.program_id(1)
    H, hd = q_ref.shape
    ps = k_refs[0].shape[0]
    flat = ps * H

    @pl.when(j == 0)
    def _():
        acc_ref[...] = jnp.zeros_like(acc_ref)
        carry_ref[...] = jnp.zeros_like(carry_ref)

    own = (lax.broadcasted_iota(jnp.int32, (H, flat), 1) % H) == lax.broadcasted_iota(jnp.int32, (H, flat), 0)
    q = q_ref[...]
    nt = (((1,), (1,)), ((), ()))
    raw = [jnp.where(own, lax.dot_general(q, k_refs[g][...].reshape(flat, hd), nt,
                                          preferred_element_type=F32), 0.0) for g in range(G)]
    raw = jnp.concatenate(raw, axis=0)
    hi, mid, lo = _split3(raw)
    pick = pick_ref[...]
    z = (jnp.dot(hi, pick, preferred_element_type=F32) + jnp.dot(mid, pick, preferred_element_type=F32)
         + jnp.dot(lo, pick, preferred_element_type=F32))
    z = z * scale + bias_ref[...]
    core = _softplus_core(z)
    log_beta = jnp.minimum(z, 0.0) - core
    log_1m = jnp.minimum(-z, 0.0) - core
    l_hi, l_lo = _split2(log_1m)
    later_m = _later_matrix(ps)
    later = jnp.dot(l_hi, later_m, preferred_element_type=F32) + jnp.dot(l_lo, later_m, preferred_element_type=F32)
    total = later[:, 0:1] + log_1m[:, 0:1]
    carry = carry_ref[...]
    betweens = []
    for g in range(G):
        rs = slice(g * H, (g + 1) * H)
        betweens.append(later[rs, :] + carry)
        carry = carry + total[rs, :]
    carry_ref[...] = carry
    w = jnp.exp(log_beta + jnp.concatenate(betweens, axis=0)).astype(BF16)
    wide = jnp.dot(w, spread_ref[...], preferred_element_type=F32)
    acc = acc_ref[...]
    for g in range(G):
        wg = jnp.where(own, wide[g * H:(g + 1) * H, :], 0.0)
        acc = acc + jnp.dot(wg, v_refs[g][...].reshape(flat, hd), preferred_element_type=F32)
    acc_ref[...] = acc

    @pl.when(j == n_steps - 1)
    def _():
        o_ref[...] = acc


def _sb_sample(q, bias, cache_k, cache_v, page_table, layer):
    B, H, hd = q.shape
    n_pages = page_table.shape[1]
    ps = cache_k.shape[2]
    G = SB_PAGES_PER_STEP
    assert n_pages % G == 0
    n_steps = n_pages // G
    flat = ps * H
    pick = (jnp.arange(flat)[:, None] // H == jnp.arange(ps)[None, :]).astype(BF16)
    spread = pick.T
    bias_t = jnp.broadcast_to(jnp.tile(bias.astype(F32), G)[:, None], (G * H, ps))

    def page_spec(g):
        return pl.BlockSpec((None, None, ps, H, hd),
                            lambda b, j, pt: (layer, pt[b, n_pages - 1 - (j * G + g)], 0, 0, 0))

    body = functools.partial(_sbs_body, scale=hd ** -0.5, n_steps=n_steps, G=G)
    grid_spec = pltpu.PrefetchScalarGridSpec(
        num_scalar_prefetch=1,
        grid=(B, n_steps),
        in_specs=[
            pl.BlockSpec((None, H, hd), lambda b, j, pt: (b, 0, 0)),
            pl.BlockSpec((G * H, ps), lambda b, j, pt: (0, 0)),
            pl.BlockSpec((flat, ps), lambda b, j, pt: (0, 0)),
            pl.BlockSpec((ps, flat), lambda b, j, pt: (0, 0)),
        ] + [page_spec(g) for g in range(G)] + [page_spec(g) for g in range(G)],
        out_specs=pl.BlockSpec((None, H, hd), lambda b, j, pt: (b, 0, 0)),
        scratch_shapes=[pltpu.VMEM((H, hd), F32), pltpu.VMEM((H, 1), F32)],
    )
    return pl.pallas_call(
        body,
        grid_spec=grid_spec,
        out_shape=jax.ShapeDtypeStruct((B, H, hd), F32),
        compiler_params=_params("parallel", "arbitrary"),
        name="sb_sample",
    )(page_table, q, bias_t, pick, spread, *([cache_k] * G), *([cache_v] * G))


def _row_tile(m, want):
    t = min(m, want)
    while m % t:
        t //= 2
    return t


def kernel(x_prompt, x_sample, cache_mem_k, cache_mem_v, state_gla, state_pool, cache_sb_k, cache_sb_v, page_table, mem_prompt, ffn_a_norm, ffn_a_wi, ffn_a_wo, mix_norm, gla_w_in, gla_w_gate, gla_b_gate, gla_out_norm, gla_w_out, sg_w_in, sg_v_norm, sg_w_s, sg_b_s, sg_w_out, pool_w, pool_scale, sb_w_qkv, sb_q_norm, sb_k_norm, sb_logit_bias, sb_w_out, mem_x_norm, mem_in_norm, mem_w_q, mem_q_norm, mem_w_k, mem_k_norm, mem_w_v, mem_w_o, ffn_b_norm, ffn_b_wi, ffn_b_wo):
    Bp, T, D = x_prompt.shape
    Bs = x_sample.shape[0]
    depth = ffn_a_norm.shape[0]
    n_mem = mem_prompt.shape[1]
    mem_w = MEM_HEADS * MEM_HD
    P = SAMPLE_PAD
    Mp, Ms = Bp * T, Bs * P

    xp = x_prompt.reshape(Mp, D)
    xs = jnp.pad(x_sample, ((0, 0), (0, P - 1), (0, 0))).reshape(Ms, D)

    tmp = _row_tile(Mp, 1024)
    tf = 256
    tn = 1024

    def first_rows(a):
        return a.reshape(Bs, P, -1)[:, 0]

    def pad_rows(a):
        return jnp.pad(a[:, None, :], ((0, 0), (0, P - 1), (0, 0))).reshape(Bs * P, -1)

    mem_flat = mem_prompt.reshape(Bp * n_mem, D)
    mem_wq_b, mem_wo_b = mem_w_q.astype(BF16), mem_w_o.astype(BF16)
    gla_w_in_b, gla_w_out_b = gla_w_in.astype(BF16), gla_w_out.astype(BF16)
    sg_w_in_b, sg_w_out_b = sg_w_in.astype(BF16), sg_w_out.astype(BF16)
    sb_w_qkv_b, sb_w_out_b = sb_w_qkv.astype(BF16), sb_w_out.astype(BF16)

    def out_proj(y, w_b, layer, res):
        return _mm(y, w_b, layer, tm=_row_tile(y.shape[0], 512), tn=w_b.shape[2], res=res)
    mem_k_new, mem_v_new = [], []
    gla_p, gla_s, pool_p, pool_s, sg_s = [], [], [], [], []
    sbk_p, sbv_p, sbk_s, sbv_s = [], [], [], []

    for i in range(depth):
        kind, j = i % 4, i // 4
        xp, xs = _ffn(xp, xs, ffn_a_norm[i], ffn_a_wi, ffn_a_wo, i, tm=tmp, tf=tf)

        if kind == 0:
            H, dk = GLA_HEADS, gla_w_gate.shape[2] // GLA_HEADS
            hk = H * dk
            hv = gla_w_out.shape[1]
            n_main = 2 * hk + 2 * hv
            w_a = jnp.pad(gla_w_in[j:j + 1, :, n_main:], ((0, 0), (0, 0), (0, LANES - GLA_RANK)))
            w_g = jnp.pad(gla_w_gate[j], ((0, LANES - GLA_RANK), (0, 0)))
            s0p = jnp.zeros((Bp, H, dk, hv // H), F32)
            outs = []
            for x, rows, c, valid, s0, tm in ((xp, T, GLA_CHUNK, GLA_CHUNK, s0p, tmp),
                                              (xs, P, P, 1, state_gla[j], Ms)):
                z = _mm(x, gla_w_in_b, j, tm=tm, tn=tn, n_cols=n_main, gain=mix_norm[i])
                a = _mm(x, w_a, 0, tm=tm, tn=LANES, gain=mix_norm[i])
                y, st = _gla(z, a, w_g, gla_b_gate[j], gla_out_norm[j], s0, rows_per_seq=rows, c=c, valid=valid)
                outs.append((out_proj(y, gla_w_out_b, j, x), st))
            (xp, st_p), (xs, st_s) = outs
            gla_p.append(st_p)
            gla_s.append(st_s)
        elif kind == 1:
            W = sg_w_out.shape[1]
            outs = []
            for x, c, nck, tm, emit_vn in ((xp, SG_CHUNK, 4, tmp, False), (xs, P, 1, Ms, True)):
                uv = _mm(x, sg_w_in_b, j, tm=tm, tn=tn, gain=mix_norm[i], act="gelu")
                b_full = jnp.repeat(sg_b_s[j][:, :c].T, W // SG_GROUPS, axis=1)
                w_s = jnp.pad(sg_w_s[j][:, :c, :c], ((0, 0), (0, SG_CHUNK - c), (0, SG_CHUNK - c)))
                res = _sg(uv, sg_v_norm[j], w_s, b_full, c=c, n_chunks=nck, emit_vn=emit_vn)
                outs.append((out_proj(res[-1], sg_w_out_b, j, x), res[0]))
            (xp, _), (xs, vn_s) = outs
            sg_s.append(first_rows(vn_s)[:, None, :])
        elif kind == 2:
            tmq = _row_tile(T, 512)
            hp, xp = _pool(xp, None, mix_norm[i], pool_w, pool_scale[j], j, rows_per_seq=T, tm=tmq, normalize=True)
            pool_p.append(hp.reshape(Bp, tmq, D)[:, tmq - POOL_BUF:])
            assert POOL_BUF + 1 == POOL_HALO
            hs = first_rows(_rmsnorm(xs, mix_norm[i]))
            hc = jnp.concatenate([state_pool[j], hs[:, None, :]], axis=1)
            res = jnp.pad(first_rows(xs)[:, None, :], ((0, 0), (POOL_BUF, 0), (0, 0)))
            _, out = _pool(hc.reshape(Bs * POOL_HALO, D), res.reshape(Bs * POOL_HALO, D), mix_norm[i],
                           pool_w, pool_scale[j], j, rows_per_seq=POOL_HALO, tm=POOL_HALO, normalize=False)
            xs = pad_rows(out.reshape(Bs, POOL_HALO, D)[:, POOL_HALO - 1])
            pool_s.append(hc[:, POOL_HALO - POOL_BUF:])
        else:
            hd = D // SB_HEADS
            hg = jnp.concatenate([jnp.tile(sb_q_norm[j], SB_HEADS), jnp.tile(sb_k_norm[j], SB_HEADS),
                                  jnp.ones((D,), F32)]).reshape(1, 3 * D)
            n_norm = 2 * D // tn
            qkv_p = _mm(xp, sb_w_qkv_b, j, tm=tmp, tn=tn, gain=mix_norm[i], headnorm=(hg, n_norm))
            att_p = _sb_prompt(qkv_p, sb_logit_bias[j], rows_per_seq=T, t=256)
            xp = out_proj(att_p, sb_w_out_b, j, xp)
            sbk_p.append(qkv_p[:, D:2 * D].reshape(Bp, T, SB_HEADS, hd))
            sbv_p.append(qkv_p[:, 2 * D:].reshape(Bp, T, SB_HEADS, hd))
            qkv_s = first_rows(_mm(xs, sb_w_qkv_b, j, tm=Ms, tn=tn, gain=mix_norm[i], headnorm=(hg, n_norm)))
            q_s, k_s, v_s = (qkv_s[:, n * D:(n + 1) * D].reshape(Bs, SB_HEADS, hd) for n in range(3))
            att_s = _sb_sample(q_s, sb_logit_bias[j], cache_sb_k, cache_sb_v, page_table, j)
            xs = out_proj(pad_rows(att_s.reshape(Bs, D)), sb_w_out_b, j, xs)
            sbk_s.append(k_s[:, None])
            sbv_s.append(v_s[:, None])

        hg_k = jnp.tile(mem_k_norm[i], MEM_HEADS).reshape(1, mem_w)
        mk = _mm(mem_flat, mem_w_k, i, tm=Bp * n_mem, tn=mem_w, gain=mem_in_norm[i], headnorm=(hg_k, 1))
        mv = _mm(mem_flat, mem_w_v, i, tm=Bp * n_mem, tn=mem_w, gain=mem_in_norm[i])
        mem_k_new.append(mk.reshape(Bp, n_mem, MEM_HEADS, MEM_HD))
        mem_v_new.append(mv.reshape(Bp, n_mem, MEM_HEADS, MEM_HD))
        xp = _mem_attend(xp, mem_x_norm[i], mem_wq_b, mem_q_norm[i], mk.reshape(Bp, n_mem, mem_w),
                         mv.reshape(Bp, n_mem, mem_w), mem_wo_b, i, rows_per_seq=T, tm=_row_tile(T, 512))
        xs = _mem_attend(xs, mem_x_norm[i], mem_wq_b, mem_q_norm[i], cache_mem_k[i].reshape(Bs, n_mem, mem_w),
                         cache_mem_v[i].reshape(Bs, n_mem, mem_w), mem_wo_b, i, rows_per_seq=P, tm=P)

        xp, xs = _ffn(xp, xs, ffn_b_norm[i], ffn_b_wi, ffn_b_wo, i, tm=tmp, tf=tf)

    return (xp.reshape(Bp, T, D), first_rows(xs)[:, None, :], jnp.stack(mem_k_new), jnp.stack(mem_v_new),
            jnp.stack(gla_p), jnp.stack(gla_s), jnp.stack(pool_p), jnp.stack(pool_s), jnp.stack(sg_s),
            jnp.stack(sbk_p), jnp.stack(sbv_p), jnp.stack(sbk_s), jnp.stack(sbv_s))
```

```python
import functools

import jax
import jax.numpy as jnp
from jax import lax
from jax.experimental import pallas as pl
from jax.experimental.pallas import tpu as pltpu

F32 = jnp.float32
BF16 = jnp.bfloat16

EPS = 1e-6
LANES = 128
SAMPLE_PAD = 8
VMEM_LIMIT_BYTES = 56 * 1024 * 1024

MEM_HEADS = 4
MEM_HD = 128
GLA_HEADS = 4
GLA_RANK = 16
GLA_TAU = 16.0
GLA_CHUNK = 64
SG_CHUNK = 128
SG_GROUPS = 8
POOL_WINDOWS = (2, 4, 8, 16)
POOL_BUF = max(POOL_WINDOWS) - 1
POOL_HALO = 16
SB_HEADS = 16
SB_HEADS_PER_STEP = 4
SB_PAGES_PER_STEP = 8


def _params(*sem):
    return pltpu.CompilerParams(dimension_semantics=sem, vmem_limit_bytes=VMEM_LIMIT_BYTES)


def _rms(x, g):
    ms = jnp.mean(x * x, axis=-1, keepdims=True)
    return x * lax.rsqrt(ms + EPS) * g


def _bdot(a, b):
    return jnp.dot(a.astype(BF16), b.astype(BF16), preferred_element_type=F32)


def _bdot_nt(a, b):
    return lax.dot_general(a.astype(BF16), b.astype(BF16), (((1,), (1,)), ((), ())),
                           preferred_element_type=F32)


def _softplus_core(z):
    return jnp.log(1.0 + jnp.exp(-jnp.abs(z)))


LOG2E = 1.4426950408889634


def _split2(x):
    hi = x.astype(BF16)
    return hi, (x - hi.astype(F32)).astype(BF16)


def _split3(x):
    hi = x.astype(BF16)
    r1 = x - hi.astype(F32)
    mid = r1.astype(BF16)
    lo = (r1 - mid.astype(F32)).astype(BF16)
    return hi, mid, lo


def _later_matrix(n):
    return (lax.broadcasted_iota(jnp.int32, (n, n), 0) >
            lax.broadcasted_iota(jnp.int32, (n, n), 1)).astype(BF16)


def _row(a):
    return a.reshape(1, a.shape[-1])


def _ffn_body(x_ref, xs_ref, g_ref, wg_ref, wu_ref, wo_ref, o_ref, os_ref, h_ref, *, n_ff):
    i = pl.program_id(0)
    j = pl.program_id(1)
    tm = x_ref.shape[0]
    ms = xs_ref.shape[0]

    def half_step(with_sample):
        @pl.when(j == 0)
        def _():
            h_ref[:tm, :] = _rms(x_ref[...], g_ref[...]).astype(BF16)
            o_ref[...] = jnp.zeros_like(o_ref)
            if with_sample:
                h_ref[tm:, :] = _rms(xs_ref[...], g_ref[...]).astype(BF16)
                os_ref[...] = jnp.zeros_like(os_ref)

        h = h_ref[...] if with_sample else h_ref[:tm, :]
        g = jnp.dot(h, wg_ref[...].astype(BF16), preferred_element_type=F32)
        u = jnp.dot(h, wu_ref[...].astype(BF16), preferred_element_type=F32)
        a = (g * jax.nn.sigmoid(g) * u).astype(BF16)
        down = jnp.dot(a, wo_ref[...].astype(BF16), preferred_element_type=F32)
        o_ref[...] += down[:tm, :]
        if with_sample:
            os_ref[...] += down[tm:, :]

        @pl.when(j == n_ff - 1)
        def _():
            o_ref[...] = x_ref[...] + 0.5 * o_ref[...]
            if with_sample:
                os_ref[...] = xs_ref[...] + 0.5 * os_ref[...]

    @pl.when(i == 0)
    def _():
        half_step(True)

    @pl.when(i > 0)
    def _():
        half_step(False)


def _ffn(x, xs, gain, wi, wo, layer, *, tm, tf):
    M, D = x.shape
    Ms = xs.shape[0]
    F = wo.shape[1]
    n_ff = F // tf
    assert M % tm == 0 and F % tf == 0
    return pl.pallas_call(
        functools.partial(_ffn_body, n_ff=n_ff),
        grid=(M // tm, n_ff),
        in_specs=[
            pl.BlockSpec((tm, D), lambda i, j: (i, 0)),
            pl.BlockSpec((Ms, D), lambda i, j: (0, 0)),
            pl.BlockSpec((1, D), lambda i, j: (0, 0)),
            pl.BlockSpec((None, D, tf), lambda i, j: (layer, 0, j)),
            pl.BlockSpec((None, D, tf), lambda i, j: (layer, 0, j + n_ff)),
            pl.BlockSpec((None, tf, D), lambda i, j: (layer, j, 0)),
        ],
        out_specs=[pl.BlockSpec((tm, D), lambda i, j: (i, 0)), pl.BlockSpec((Ms, D), lambda i, j: (0, 0))],
        out_shape=[jax.ShapeDtypeStruct((M, D), F32), jax.ShapeDtypeStruct((Ms, D), F32)],
        scratch_shapes=[pltpu.VMEM((tm + Ms, D), BF16)],
        compiler_params=_params("arbitrary", "arbitrary"),
        name="ffn",
    )(x, xs, _row(gain), wi, wi, wo)


def _mm_body(*refs, has_gain, act, has_res, n_headnorm):
    it = iter(refs)
    x_ref = next(it)
    g_ref = next(it) if has_gain else None
    w_ref = next(it)
    hg_ref = next(it) if n_headnorm else None
    r_ref = next(it) if has_res else None
    o_ref = next(it)
    h_ref = next(it)
    j = pl.program_id(1)

    @pl.when(j == 0)
    def _():
        x = x_ref[...]
        if has_gain:
            x = _rms(x, g_ref[...])
        h_ref[...] = x.astype(BF16)

    y = jnp.dot(h_ref[...], w_ref[...].astype(BF16), preferred_element_type=F32)
    if act == "gelu":
        y = jax.nn.gelu(y, approximate=True)
    if has_res:
        y = r_ref[...] + y
    o_ref[...] = y

    if n_headnorm:
        @pl.when(j < n_headnorm)
        def _():
            tn = o_ref.shape[1]
            for s in range(tn // LANES):
                sl = slice(s * LANES, (s + 1) * LANES)
                o_ref[:, sl] = _rms(y[:, sl], hg_ref[:, sl])


def _mm(x, w, layer, *, tm, tn, n_cols=None, gain=None, act=None, res=None, headnorm=None):
    M, K = x.shape
    N = n_cols if n_cols is not None else w.shape[2]
    assert M % tm == 0 and N % tn == 0
    args, specs = [x], [pl.BlockSpec((tm, K), lambda i, j: (i, 0))]
    if gain is not None:
        args.append(_row(gain))
        specs.append(pl.BlockSpec((1, K), lambda i, j: (0, 0)))
    args.append(w)
    specs.append(pl.BlockSpec((None, K, tn), lambda i, j: (layer, 0, j)))
    n_headnorm = 0
    if headnorm is not None:
        hg, n_headnorm = headnorm
        args.append(hg)
        specs.append(pl.BlockSpec((1, tn), lambda i, j: (0, j)))
    if res is not None:
        args.append(res)
        specs.append(pl.BlockSpec((tm, tn), lambda i, j: (i, j)))
    return pl.pallas_call(
        functools.partial(_mm_body, has_gain=gain is not None, act=act, has_res=res is not None,
                          n_headnorm=n_headnorm),
        grid=(M // tm, N // tn),
        in_specs=specs,
        out_specs=pl.BlockSpec((tm, tn), lambda i, j: (i, j)),
        out_shape=jax.ShapeDtypeStruct((M, N), F32),
        scratch_shapes=[pltpu.VMEM((tm, K), BF16)],
        compiler_params=_params("parallel", "arbitrary"),
        name="mm",
    )(*args)


def _memattn_body(x_ref, gx_ref, wq_ref, gq_ref, k_ref, v_ref, wo_ref, o_ref):
    x = x_ref[...]
    h = _rms(x, gx_ref[...])
    q = _bdot(h, wq_ref[...])
    scale = MEM_HD ** -0.5
    outs = []
    for hd in range(MEM_HEADS):
        sl = slice(hd * MEM_HD, (hd + 1) * MEM_HD)
        qh = _rms(q[:, sl], gq_ref[...])
        s = _bdot_nt(qh, k_ref[:, sl]) * scale
        s = s - jnp.max(s, axis=-1, keepdims=True)
        e = jnp.exp(s)
        p = e / jnp.sum(e, axis=-1, keepdims=True)
        outs.append(_bdot(p, v_ref[:, sl]))
    o = jnp.concatenate(outs, axis=-1)
    o_ref[...] = x + _bdot(o, wo_ref[...])


def _mem_attend(x, gx, wq, gq, k, v, wo, layer, *, rows_per_seq, tm):
    M, D = x.shape
    B, N, W = k.shape
    nt = rows_per_seq // tm
    assert rows_per_seq % tm == 0 and M == B * rows_per_seq
    return pl.pallas_call(
        _memattn_body,
        grid=(B, nt),
        in_specs=[
            pl.BlockSpec((tm, D), lambda b, i: (b * nt + i, 0)),
            pl.BlockSpec((1, D), lambda b, i: (0, 0)),
            pl.BlockSpec((None, D, W), lambda b, i: (layer, 0, 0)),
            pl.BlockSpec((1, MEM_HD), lambda b, i: (0, 0)),
            pl.BlockSpec((None, N, W), lambda b, i: (b, 0, 0)),
            pl.BlockSpec((None, N, W), lambda b, i: (b, 0, 0)),
            pl.BlockSpec((None, W, D), lambda b, i: (layer, 0, 0)),
        ],
        out_specs=pl.BlockSpec((tm, D), lambda b, i: (b * nt + i, 0)),
        out_shape=jax.ShapeDtypeStruct((M, D), F32),
        compiler_params=_params("parallel", "arbitrary"),
        name="mem_attend",
    )(x, _row(gx), wq, _row(gq), k, v, wo)


def _gla_body(q_ref, k_ref, v_ref, r_ref, a_ref, wg_ref, bg_ref, go_ref, s0_ref, y_ref, s_ref,
              *, c, valid, dk, dv):
    i = pl.program_id(1)

    @pl.when(i == 0)
    def _():
        s_ref[...] = s0_ref[...]

    R = LANES
    rows = lax.broadcasted_iota(jnp.int32, (R, 1), 0)
    live = rows < valid
    tri_r = (lax.broadcasted_iota(jnp.int32, (R, R), 1) <= lax.broadcasted_iota(jnp.int32, (R, R), 0))
    tri_b = tri_r.astype(BF16)
    tri_c = tri_r[:c, :]

    def pad(x):
        return jnp.concatenate([x, jnp.zeros((R - c, x.shape[1]), F32)], axis=0)

    heads = range(GLA_HEADS)
    kss = [slice(hd * dk, (hd + 1) * dk) for hd in heads]
    vss = [slice(hd * dv, (hd + 1) * dv) for hd in heads]
    for u in range(q_ref.shape[0] // c):
        rs = slice(u * c, (u + 1) * c)
        a = pad(a_ref[rs, :])
        las = [_bdot(a, wg_ref[:, ks]) + bg_ref[:, ks] for ks in kss]
        parts = []
        for la in las:
            la = (jnp.minimum(la, 0.0) - _softplus_core(la)) / GLA_TAU
            parts.append(jnp.concatenate(_split3(jnp.where(live, la, 0.0)), axis=1))
        bs = []
        for p in parts:
            c3 = jnp.dot(tri_b, p, preferred_element_type=F32)
            bs.append(c3[:, :dk] + c3[:, dk:2 * dk] + c3[:, 2 * dk:])
        vs_, qds, kds, sts = [], [], [], []
        for hd in heads:
            b = bs[hd]
            b_last = b[R - 1:R, :]
            k = jnp.where(live, pad(k_ref[rs, kss[hd]]), 0.0)
            vs_.append(jnp.where(live, pad(v_ref[rs, vss[hd]]), 0.0))
            qds.append(q_ref[rs, kss[hd]] * (dk ** -0.5) * jnp.exp(b[:c, :]))
            kds.append(k * jnp.exp(-b))
            sts.append(jnp.where(rows < R - 1, k * jnp.exp(b_last - b), jnp.exp(b_last)))
        atts = [jnp.where(tri_c, _bdot_nt(qds[hd], kds[hd]), 0.0) for hd in heads]
        states = [s_ref[0, hd] for hd in heads]
        os_ = [_bdot(atts[hd], vs_[hd]) + _bdot(qds[hd], states[hd]) for hd in heads]
        for hd in heads:
            st = sts[hd].T
            s_ref[0, hd] = st[:, R - 1:R] * states[hd] + _bdot(st, vs_[hd])
        for hd in heads:
            r = r_ref[rs, vss[hd]]
            y_ref[rs, vss[hd]] = _rms(os_[hd], go_ref[...]) * (r * jax.nn.sigmoid(r))


def _gla(z, a, w_gate_pad, b_gate, g_out, s0, *, rows_per_seq, c, valid):
    M = z.shape[0]
    B, H, dk, dv = s0.shape
    hk, hv = H * dk, H * dv
    rb = c
    n = rows_per_seq // rb
    assert rows_per_seq % rb == 0 and c < LANES and hv == 2 * hk
    body = functools.partial(_gla_body, c=c, valid=valid, dk=dk, dv=dv)
    return pl.pallas_call(
        body,
        grid=(B, n),
        in_specs=[
            pl.BlockSpec((rb, hk), lambda b, i: (b * n + i, 0)),
            pl.BlockSpec((rb, hk), lambda b, i: (b * n + i, 1)),
            pl.BlockSpec((rb, hv), lambda b, i: (b * n + i, 1)),
            pl.BlockSpec((rb, hv), lambda b, i: (b * n + i, 2)),
            pl.BlockSpec((rb, LANES), lambda b, i: (b * n + i, 0)),
            pl.BlockSpec((LANES, hk), lambda b, i: (0, 0)),
            pl.BlockSpec((1, hk), lambda b, i: (0, 0)),
            pl.BlockSpec((1, dv), lambda b, i: (0, 0)),
            pl.BlockSpec((1, H, dk, dv), lambda b, i: (b, 0, 0, 0)),
        ],
        out_specs=[
            pl.BlockSpec((rb, hv), lambda b, i: (b * n + i, 0)),
            pl.BlockSpec((1, H, dk, dv), lambda b, i: (b, 0, 0, 0)),
        ],
        out_shape=[jax.ShapeDtypeStruct((M, hv), F32), jax.ShapeDtypeStruct((B, H, dk, dv), F32)],
        compiler_params=_params("parallel", "arbitrary"),
        name="gla",
    )(z, z, z, z, a, w_gate_pad, b_gate.reshape(1, hk), g_out.reshape(1, dv), s0)


def _sg_body(u_ref, v_ref, gv_ref, ws_ref, bs_ref, *out_refs, c, n_chunks, gw):
    y_ref = out_refs[-1]
    R = ws_ref.shape[1]
    tri = (lax.broadcasted_iota(jnp.int32, (R, R), 1) <= lax.broadcasted_iota(jnp.int32, (R, R), 0))
    vn = _rms(v_ref[...], gv_ref[...])
    if len(out_refs) == 2:
        out_refs[0][...] = vn
    for g in range(SG_GROUPS):
        w = jnp.where(tri, ws_ref[g], 0.0).astype(BF16)
        gs = slice(g * gw, (g + 1) * gw)
        for t in range(n_chunks):
            rs = slice(t * c, (t + 1) * c)
            vc = vn[rs, gs]
            if c < R:
                vc = jnp.concatenate([vc, jnp.zeros((R - c, gw), F32)], axis=0)
            mixed = jnp.dot(w, vc.astype(BF16), preferred_element_type=F32)[:c, :] + bs_ref[:, gs]
            y_ref[rs, gs] = u_ref[rs, gs] * mixed


def _sg(uv, g_v, w_s, b_full, *, c, n_chunks, emit_vn):
    M, W2 = uv.shape
    W = W2 // 2
    tm = c * n_chunks
    assert M % tm == 0
    body = functools.partial(_sg_body, c=c, n_chunks=n_chunks, gw=W // SG_GROUPS)
    n_out = 2 if emit_vn else 1
    return pl.pallas_call(
        body,
        grid=(M // tm,),
        in_specs=[
            pl.BlockSpec((tm, W), lambda i: (i, 0)),
            pl.BlockSpec((tm, W), lambda i: (i, 1)),
            pl.BlockSpec((1, W), lambda i: (0, 0)),
            pl.BlockSpec(w_s.shape, lambda i: (0, 0, 0)),
            pl.BlockSpec((c, W), lambda i: (0, 0)),
        ],
        out_specs=[pl.BlockSpec((tm, W), lambda i: (i, 0))] * n_out,
        out_shape=[jax.ShapeDtypeStruct((M, W), F32)] * n_out,
        compiler_params=_params("parallel"),
        name="sg",
    )(uv, uv, g_v.reshape(1, W), w_s, b_full)


def _rmsnorm_body(x_ref, g_ref, o_ref):
    o_ref[...] = _rms(x_ref[...], g_ref[...])


def _rmsnorm(x, gain):
    M, D = x.shape
    return pl.pallas_call(
        _rmsnorm_body,
        grid=(1,),
        in_specs=[pl.BlockSpec((M, D), lambda i: (0, 0)), pl.BlockSpec((1, D), lambda i: (0, 0))],
        out_specs=pl.BlockSpec((M, D), lambda i: (0, 0)),
        out_shape=jax.ShapeDtypeStruct((M, D), F32),
        name="rmsnorm",
    )(x, gain.reshape(1, D))


def _pool_body(*refs, tm, gw, normalize, has_res):
    if has_res:
        x_ref, halo_ref, res_ref, g_ref, wp_ref, sc_ref, h_ref, o_ref = refs
    else:
        x_ref, halo_ref, g_ref, wp_ref, sc_ref, h_ref, o_ref = refs
        res_ref = x_ref
    i = pl.program_id(1)
    h, halo = x_ref[...], halo_ref[...]
    if normalize:
        h, halo = _rms(h, g_ref[...]), _rms(halo, g_ref[...])
    h_ref[...] = h
    halo = jnp.where(i > 0, halo, 0.0)
    hc = jnp.concatenate([halo, h], axis=0)
    pos = i * tm + lax.broadcasted_iota(jnp.int32, (tm, 1), 0)
    for g, w in enumerate(POOL_WINDOWS):
        gs = slice(g * gw, (g + 1) * gw)
        s = hc[:, gs]
        span, n = 1, s.shape[0]
        while span < w:
            s = s[span:, :] + s[:n - span, :]
            n -= span
            span *= 2
        win = s[n - tm:, :]
        cnt = jnp.minimum(pos + 1, w).astype(F32)
        pooled = win / cnt - h[:, gs]
        o_ref[:, gs] = res_ref[:, gs] + _bdot(pooled, wp_ref[g]) * sc_ref[:, gs]


def _pool(x, res, gain, w_pool, scale, layer, *, rows_per_seq, tm, normalize):
    M, D = x.shape
    B = M // rows_per_seq
    nt = rows_per_seq // tm
    hb = tm // POOL_HALO
    G = len(POOL_WINDOWS)
    gw = D // G
    assert rows_per_seq % tm == 0 and tm % POOL_HALO == 0
    body = functools.partial(_pool_body, tm=tm, gw=gw, normalize=normalize, has_res=res is not None)
    row_spec = pl.BlockSpec((tm, D), lambda b, i: (b * nt + i, 0))
    args = [x, x] + ([res] if res is not None else []) + [gain.reshape(1, D), w_pool, scale.reshape(1, D)]
    specs = ([row_spec, pl.BlockSpec((POOL_HALO, D), lambda b, i: (jnp.maximum((b * nt + i) * hb - 1, 0), 0))]
             + ([row_spec] if res is not None else [])
             + [pl.BlockSpec((1, D), lambda b, i: (0, 0)),
                pl.BlockSpec((None, G, gw, gw), lambda b, i: (layer, 0, 0, 0)),
                pl.BlockSpec((1, D), lambda b, i: (0, 0))])
    return pl.pallas_call(
        body,
        grid=(B, nt),
        in_specs=specs,
        out_specs=[pl.BlockSpec((tm, D), lambda b, i: (b, 0)), row_spec],
        out_shape=[jax.ShapeDtypeStruct((B * tm, D), F32), jax.ShapeDtypeStruct((M, D), F32)],
        compiler_params=_params("arbitrary", "arbitrary"),
        name="pool",
    )(*args)


def _sb_body(bias_ref, q_ref, k_ref, v_ref, o_ref, carry_ref, *, t, nh, scale):
    hp = pl.program_id(1)
    qi = pl.program_id(2)
    later_m = _later_matrix(t)
    causal = (lax.broadcasted_iota(jnp.int32, (t, t), 1) < lax.broadcasted_iota(jnp.int32, (t, t), 0))

    heads = [slice(h * LANES, (h + 1) * LANES) for h in range(nh)]

    def tiles(starts, diagonal):
        chains = [(s, h) for s in range(len(starts)) for h in range(nh)]
        zs = [_bdot_nt(q_ref[:, heads[h]], k_ref[pl.ds(starts[s], t), heads[h]]) for s, h in chains]
        log_betas, log_1ms = [], []
        for (s, h), z in zip(chains, zs):
            z2 = z * (scale * LOG2E) + bias_ref[hp * nh + h] * LOG2E
            core = jnp.log2(1.0 + jnp.exp2(-jnp.abs(z2)))
            log_beta = jnp.minimum(z2, 0.0) - core
            log_1m = log_beta - z2
            if diagonal and s == 0:
                log_1m = jnp.where(causal, log_1m, 0.0)
            log_betas.append(log_beta)
            log_1ms.append(log_1m)
        laters = [jnp.dot(l.astype(BF16), later_m, preferred_element_type=F32) for l in log_1ms]
        ws = []
        for s, h in chains:
            n = s * nh + h
            first = diagonal and s == 0
            between = laters[n] if first else laters[n] + carry_ref[h]
            carry_ref[h] = between[:, 0:1] + log_1ms[n][:, 0:1]
            w = jnp.exp2(log_betas[n] + between)
            ws.append(jnp.where(causal, w, 0.0) if first else w)
        for h in range(nh):
            pv = _bdot(ws[h], v_ref[pl.ds(starts[0], t), heads[h]])
            for s in range(1, len(starts)):
                pv = pv + _bdot(ws[s * nh + h], v_ref[pl.ds(starts[s], t), heads[h]])
            if diagonal:
                o_ref[:, heads[h]] = pv
            else:
                o_ref[:, heads[h]] += pv

    def at(tile_index):
        return pl.multiple_of(tile_index * t, t)

    odd = qi % 2

    @pl.when(odd == 0)
    def _():
        tiles([at(qi)], True)

    @pl.when(odd == 1)
    def _():
        tiles([at(qi), at(qi - 1)], True)

    def step(it, _):
        newest = qi - odd - 1 - 2 * it
        tiles([at(newest), at(newest - 1)], False)
        return 0

    lax.fori_loop(0, (qi - odd) // 2, step, 0)


def _sb_prompt(qkv, bias, *, rows_per_seq, t):
    M, D3 = qkv.shape
    D = D3 // 3
    hd = D // SB_HEADS
    nh = SB_HEADS_PER_STEP
    B = M // rows_per_seq
    nq = rows_per_seq // t
    groups = SB_HEADS // nh
    assert hd == LANES and rows_per_seq % t == 0 and SB_HEADS % nh == 0
    body = functools.partial(_sb_body, t=t, nh=nh, scale=hd ** -0.5)
    return pl.pallas_call(
        body,
        grid=(B, groups, nq),
        in_specs=[
            pl.BlockSpec(memory_space=pltpu.SMEM),
            pl.BlockSpec((t, nh * hd), lambda b, g, i: (b * nq + i, g)),
            pl.BlockSpec((rows_per_seq, nh * hd), lambda b, g, i: (b, groups + g)),
            pl.BlockSpec((rows_per_seq, nh * hd), lambda b, g, i: (b, 2 * groups + g)),
        ],
        out_specs=pl.BlockSpec((t, nh * hd), lambda b, g, i: (b * nq + i, g)),
        out_shape=jax.ShapeDtypeStruct((M, D), F32),
        scratch_shapes=[pltpu.VMEM((nh, t, 1), F32)],
        compiler_params=_params("parallel", "parallel", "arbitrary"),
        name="sb_prompt",
    )(bias, qkv, qkv, qkv)


def _sbs_body(pt_ref, q_ref, bias_ref, pick_ref, spread_ref, *refs, scale, n_steps, G):
    k_refs, v_refs = refs[:G], refs[G:2 * G]
    o_ref, acc_ref, carry_ref = refs[2 * G:]
    j = pl.program_id(1)
    H, hd = q_ref.shape
    ps = k_refs[0].shape[0]
    flat = ps * H

    @pl.when(j == 0)
    def _():
        acc_ref[...] = jnp.zeros_like(acc_ref)
        carry_ref[...] = jnp.zeros_like(carry_ref)

    own = (lax.broadcasted_iota(jnp.int32, (H, flat), 1) % H) == lax.broadcasted_iota(jnp.int32, (H, flat), 0)
    q = q_ref[...]
    nt = (((1,), (1,)), ((), ()))
    raw = [jnp.where(own, lax.dot_general(q, k_refs[g][...].reshape(flat, hd), nt,
                                          preferred_element_type=F32), 0.0) for g in range(G)]
    raw = jnp.concatenate(raw, axis=0)
    hi, mid, lo = _split3(raw)
    pick = pick_ref[...]
    z = (jnp.dot(hi, pick, preferred_element_type=F32) + jnp.dot(mid, pick, preferred_element_type=F32)
         + jnp.dot(lo, pick, preferred_element_type=F32))
    z = z * scale + bias_ref[...]
    core = _softplus_core(z)
    log_beta = jnp.minimum(z, 0.0) - core
    log_1m = jnp.minimum(-z, 0.0) - core
    l_hi, l_lo = _split2(log_1m)
    later_m = _later_matrix(ps)
    later = jnp.dot(l_hi, later_m, preferred_element_type=F32) + jnp.dot(l_lo, later_m, preferred_element_type=F32)
    total = later[:, 0:1] + log_1m[:, 0:1]
    carry = carry_ref[...]
    betweens = []
    for g in range(G):
        rs = slice(g * H, (g + 1) * H)
        betweens.append(later[rs, :] + carry)
        carry = carry + total[rs, :]
    carry_ref[...] = carry
    w = jnp.exp(log_beta + jnp.concatenate(betweens, axis=0)).astype(BF16)
    wide = jnp.dot(w, spread_ref[...], preferred_element_type=F32)
    acc = acc_ref[...]
    for g in range(G):
        wg = jnp.where(own, wide[g * H:(g + 1) * H, :], 0.0)
        acc = acc + jnp.dot(wg, v_refs[g][...].reshape(flat, hd), preferred_element_type=F32)
    acc_ref[...] = acc

    @pl.when(j == n_steps - 1)
    def _():
        o_ref[...] = acc


def _sb_sample(q, bias, cache_k, cache_v, page_table, layer):
    B, H, hd = q.shape
    n_pages = page_table.shape[1]
    ps = cache_k.shape[2]
    G = SB_PAGES_PER_STEP
    assert n_pages % G == 0
    n_steps = n_pages // G
    flat = ps * H
    pick = (jnp.arange(flat)[:, None] // H == jnp.arange(ps)[None, :]).astype(BF16)
    spread = pick.T
    bias_t = jnp.broadcast_to(jnp.tile(bias.astype(F32), G)[:, None], (G * H, ps))

    def page_spec(g):
        return pl.BlockSpec((None, None, ps, H, hd),
                            lambda b, j, pt: (layer, pt[b, n_pages - 1 - (j * G + g)], 0, 0, 0))

    body = functools.partial(_sbs_body, scale=hd ** -0.5, n_steps=n_steps, G=G)
    grid_spec = pltpu.PrefetchScalarGridSpec(
        num_scalar_prefetch=1,
        grid=(B, n_steps),
        in_specs=[
            pl.BlockSpec((None, H, hd), lambda b, j, pt: (b, 0, 0)),
            pl.BlockSpec((G * H, ps), lambda b, j, pt: (0, 0)),
            pl.BlockSpec((flat, ps), lambda b, j, pt: (0, 0)),
            pl.BlockSpec((ps, flat), lambda b, j, pt: (0, 0)),
        ] + [page_spec(g) for g in range(G)] + [page_spec(g) for g in range(G)],
        out_specs=pl.BlockSpec((None, H, hd), lambda b, j, pt: (b, 0, 0)),
        scratch_shapes=[pltpu.VMEM((H, hd), F32), pltpu.VMEM((H, 1), F32)],
    )
    return pl.pallas_call(
        body,
        grid_spec=grid_spec,
        out_shape=jax.ShapeDtypeStruct((B, H, hd), F32),
        compiler_params=_params("parallel", "arbitrary"),
        name="sb_sample",
    )(page_table, q, bias_t, pick, spread, *([cache_k] * G), *([cache_v] * G))


def _row_tile(m, want):
    t = min(m, want)
    while m % t:
        t //= 2
    return t


def kernel(x_prompt, x_sample, cache_mem_k, cache_mem_v, state_gla, state_pool, cache_sb_k, cache_sb_v, page_table, mem_prompt, ffn_a_norm, ffn_a_wi, ffn_a_wo, mix_norm, gla_w_in, gla_w_gate, gla_b_gate, gla_out_norm, gla_w_out, sg_w_in, sg_v_norm, sg_w_s, sg_b_s, sg_w_out, pool_w, pool_scale, sb_w_qkv, sb_q_norm, sb_k_norm, sb_logit_bias, sb_w_out, mem_x_norm, mem_in_norm, mem_w_q, mem_q_norm, mem_w_k, mem_k_norm, mem_w_v, mem_w_o, ffn_b_norm, ffn_b_wi, ffn_b_wo):
    Bp, T, D = x_prompt.shape
    Bs = x_sample.shape[0]
    depth = ffn_a_norm.shape[0]
    n_mem = mem_prompt.shape[1]
    mem_w = MEM_HEADS * MEM_HD
    P = SAMPLE_PAD
    Mp, Ms = Bp * T, Bs * P

    xp = x_prompt.reshape(Mp, D)
    xs = jnp.pad(x_sample, ((0, 0), (0, P - 1), (0, 0))).reshape(Ms, D)

    tmp = _row_tile(Mp, 1024)
    tf = 256
    tn = 1024

    def first_rows(a):
        return a.reshape(Bs, P, -1)[:, 0]

    def pad_rows(a):
        return jnp.pad(a[:, None, :], ((0, 0), (0, P - 1), (0, 0))).reshape(Bs * P, -1)

    mem_flat = mem_prompt.reshape(Bp * n_mem, D)
    mem_wq_b, mem_wo_b = mem_w_q.astype(BF16), mem_w_o.astype(BF16)
    gla_w_in_b, gla_w_out_b = gla_w_in.astype(BF16), gla_w_out.astype(BF16)
    sg_w_in_b, sg_w_out_b = sg_w_in.astype(BF16), sg_w_out.astype(BF16)
    sb_w_qkv_b, sb_w_out_b = sb_w_qkv.astype(BF16), sb_w_out.astype(BF16)

    def out_proj(y, w_b, layer, res):
        return _mm(y, w_b, layer, tm=_row_tile(y.shape[0], 512), tn=w_b.shape[2], res=res)
    mem_k_new, mem_v_new = [], []
    gla_p, gla_s, pool_p, pool_s, sg_s = [], [], [], [], []
    sbk_p, sbv_p, sbk_s, sbv_s = [], [], [], []

    for i in range(depth):
        kind, j = i % 4, i // 4
        xp, xs = _ffn(xp, xs, ffn_a_norm[i], ffn_a_wi, ffn_a_wo, i, tm=tmp, tf=tf)

        if kind == 0:
            H, dk = GLA_HEADS, gla_w_gate.shape[2] // GLA_HEADS
            hk = H * dk
            hv = gla_w_out.shape[1]
            n_main = 2 * hk + 2 * hv
            w_a = jnp.pad(gla_w_in_b[j:j + 1, :, n_main:], ((0, 0), (0, 0), (0, LANES - GLA_RANK)))
            w_g = jnp.pad(gla_w_gate[j], ((0, LANES - GLA_RANK), (0, 0)))
            s0p = jnp.zeros((Bp, H, dk, hv // H), F32)
            outs = []
            for x, rows, c, valid, s0, tm in ((xp, T, GLA_CHUNK, GLA_CHUNK, s0p, tmp),
                                              (xs, P, P, 1, state_gla[j], Ms)):
                z = _mm(x, gla_w_in_b, j, tm=tm, tn=tn, n_cols=n_main, gain=mix_norm[i])
                a = _mm(x, w_a, 0, tm=tm, tn=LANES, gain=mix_norm[i])
                y, st = _gla(z, a, w_g, gla_b_gate[j], gla_out_norm[j], s0, rows_per_seq=rows, c=c, valid=valid)
                outs.append((out_proj(y, gla_w_out_b, j, x), st))
            (xp, st_p), (xs, st_s) = outs
            gla_p.append(st_p)
            gla_s.append(st_s)
        elif kind == 1:
            W = sg_w_out.shape[1]
            outs = []
            for x, c, nck, tm, emit_vn in ((xp, SG_CHUNK, 4, tmp, False), (xs, P, 1, Ms, True)):
                uv = _mm(x, sg_w_in_b, j, tm=tm, tn=tn, gain=mix_norm[i], act="gelu")
                b_full = jnp.repeat(sg_b_s[j][:, :c].T, W // SG_GROUPS, axis=1)
                w_s = jnp.pad(sg_w_s[j][:, :c, :c], ((0, 0), (0, SG_CHUNK - c), (0, SG_CHUNK - c)))
                res = _sg(uv, sg_v_norm[j], w_s, b_full, c=c, n_chunks=nck, emit_vn=emit_vn)
                outs.append((out_proj(res[-1], sg_w_out_b, j, x), res[0]))
            (xp, _), (xs, vn_s) = outs
            sg_s.append(first_rows(vn_s)[:, None, :])
        elif kind == 2:
            tmq = _row_tile(T, 512)
            hp, xp = _pool(xp, None, mix_norm[i], pool_w, pool_scale[j], j, rows_per_seq=T, tm=tmq, normalize=True)
            pool_p.append(hp.reshape(Bp, tmq, D)[:, tmq - POOL_BUF:])
            assert POOL_BUF + 1 == POOL_HALO
            hs = first_rows(_rmsnorm(xs, mix_norm[i]))
            hc = jnp.concatenate([state_pool[j], hs[:, None, :]], axis=1)
            res = jnp.pad(first_rows(xs)[:, None, :], ((0, 0), (POOL_BUF, 0), (0, 0)))
            _, out = _pool(hc.reshape(Bs * POOL_HALO, D), res.reshape(Bs * POOL_HALO, D), mix_norm[i],
                           pool_w, pool_scale[j], j, rows_per_seq=POOL_HALO, tm=POOL_HALO, normalize=False)
            xs = pad_rows(out.reshape(Bs, POOL_HALO, D)[:, POOL_HALO - 1])
            pool_s.append(hc[:, POOL_HALO - POOL_BUF:])
        else:
            hd = D // SB_HEADS
            hg = jnp.concatenate([jnp.tile(sb_q_norm[j], SB_HEADS), jnp.tile(sb_k_norm[j], SB_HEADS),
                                  jnp.ones((D,), F32)]).reshape(1, 3 * D)
            n_norm = 2 * D // tn
            qkv_p = _mm(xp, sb_w_qkv_b, j, tm=tmp, tn=tn, gain=mix_norm[i], headnorm=(hg, n_norm))
            att_p = _sb_prompt(qkv_p, sb_logit_bias[j], rows_per_seq=T, t=256)
            xp = out_proj(att_p, sb_w_out_b, j, xp)
            sbk_p.append(qkv_p[:, D:2 * D].reshape(Bp, T, SB_HEADS, hd))
            sbv_p.append(qkv_p[:, 2 * D:].reshape(Bp, T, SB_HEADS, hd))
            qkv_s = first_rows(_mm(xs, sb_w_qkv_b, j, tm=Ms, tn=tn, gain=mix_norm[i], headnorm=(hg, n_norm)))
            q_s, k_s, v_s = (qkv_s[:, n * D:(n + 1) * D].reshape(Bs, SB_HEADS, hd) for n in range(3))
            att_s = _sb_sample(q_s, sb_logit_bias[j], cache_sb_k, cache_sb_v, page_table, j)
            xs = out_proj(pad_rows(att_s.reshape(Bs, D)), sb_w_out_b, j, xs)
            sbk_s.append(k_s[:, None])
            sbv_s.append(v_s[:, None])

        hg_k = jnp.tile(mem_k_norm[i], MEM_HEADS).reshape(1, mem_w)
        mk = _mm(mem_flat, mem_w_k, i, tm=Bp * n_mem, tn=mem_w, gain=mem_in_norm[i], headnorm=(hg_k, 1))
        mv = _mm(mem_flat, mem_w_v, i, tm=Bp * n_mem, tn=mem_w, gain=mem_in_norm[i])
        mem_k_new.append(mk.reshape(Bp, n_mem, MEM_HEADS, MEM_HD))
        mem_v_new.append(mv.reshape(Bp, n_mem, MEM_HEADS, MEM_HD))
        xp = _mem_attend(xp, mem_x_norm[i], mem_wq_b, mem_q_norm[i], mk.reshape(Bp, n_mem, mem_w),
                         mv.reshape(Bp, n_mem, mem_w), mem_wo_b, i, rows_per_seq=T, tm=_row_tile(T, 512))
        xs = _mem_attend(xs, mem_x_norm[i], mem_wq_b, mem_q_norm[i], cache_mem_k[i].reshape(Bs, n_mem, mem_w),
                         cache_mem_v[i].reshape(Bs, n_mem, mem_w), mem_wo_b, i, rows_per_seq=P, tm=P)

        xp, xs = _ffn(xp, xs, ffn_b_norm[i], ffn_b_wi, ffn_b_wo, i, tm=tmp, tf=tf)

    return (xp.reshape(Bp, T, D), first_rows(xs)[:, None, :], jnp.stack(mem_k_new), jnp.stack(mem_v_new),
            jnp.stack(gla_p), jnp.stack(gla_s), jnp.stack(pool_p), jnp.stack(pool_s), jnp.stack(sg_s),
            jnp.stack(sbk_p), jnp.stack(sbv_p), jnp.stack(sbk_s), jnp.stack(sbv_s))
```

```python
import functools

import jax
import jax.numpy as jnp
from jax import lax
from jax.experimental import pallas as pl
from jax.experimental.pallas import tpu as pltpu

F32 = jnp.float32
BF16 = jnp.bfloat16

EPS = 1e-6
LANES = 128
SAMPLE_PAD = 8
VMEM_LIMIT_BYTES = 60 * 1024 * 1024

MEM_HEADS = 4
MEM_HD = 128
GLA_HEADS = 4
GLA_RANK = 16
GLA_TAU = 16.0
GLA_CHUNK = 64
SG_CHUNK = 128
SG_GROUPS = 8
POOL_WINDOWS = (2, 4, 8, 16)
POOL_BUF = max(POOL_WINDOWS) - 1
POOL_HALO = 16
SB_HEADS = 16
SB_HEADS_PER_STEP = 4
SB_PAGES_PER_STEP = 8


def _params(*sem):
    return pltpu.CompilerParams(dimension_semantics=sem, vmem_limit_bytes=VMEM_LIMIT_BYTES)


def _rms(x, g):
    ms = jnp.mean(x * x, axis=-1, keepdims=True)
    return x * lax.rsqrt(ms + EPS) * g


def _bdot(a, b):
    return jnp.dot(a.astype(BF16), b.astype(BF16), preferred_element_type=F32)


def _bdot_nt(a, b):
    return lax.dot_general(a.astype(BF16), b.astype(BF16), (((1,), (1,)), ((), ())),
                           preferred_element_type=F32)


def _softplus_core(z):
    return jnp.log(1.0 + jnp.exp(-jnp.abs(z)))


LOG2E = 1.4426950408889634


def _split2(x):
    hi = x.astype(BF16)
    return hi, (x - hi.astype(F32)).astype(BF16)


def _split3(x):
    hi = x.astype(BF16)
    r1 = x - hi.astype(F32)
    mid = r1.astype(BF16)
    lo = (r1 - mid.astype(F32)).astype(BF16)
    return hi, mid, lo


def _later_matrix(n):
    return (lax.broadcasted_iota(jnp.int32, (n, n), 0) >
            lax.broadcasted_iota(jnp.int32, (n, n), 1)).astype(BF16)


def _row(a):
    return a.reshape(1, a.shape[-1])


def _ffn_body(x_ref, xs_ref, g_ref, wg_ref, wu_ref, wo_ref, o_ref, os_ref, h_ref, *, n_ff):
    i = pl.program_id(0)
    j = pl.program_id(1)
    tm = x_ref.shape[0]
    ms = xs_ref.shape[0]

    def half_step(with_sample):
        @pl.when(j == 0)
        def _():
            h_ref[:tm, :] = _rms(x_ref[...], g_ref[...]).astype(BF16)
            o_ref[...] = jnp.zeros_like(o_ref)
            if with_sample:
                h_ref[tm:, :] = _rms(xs_ref[...], g_ref[...]).astype(BF16)
                os_ref[...] = jnp.zeros_like(os_ref)

        h = h_ref[...] if with_sample else h_ref[:tm, :]
        g = jnp.dot(h, wg_ref[...].astype(BF16), preferred_element_type=F32)
        u = jnp.dot(h, wu_ref[...].astype(BF16), preferred_element_type=F32)
        a = (g * jax.nn.sigmoid(g) * u).astype(BF16)
        down = jnp.dot(a, wo_ref[...].astype(BF16), preferred_element_type=F32)
        o_ref[...] += down[:tm, :]
        if with_sample:
            os_ref[...] += down[tm:, :]

        @pl.when(j == n_ff - 1)
        def _():
            o_ref[...] = x_ref[...] + 0.5 * o_ref[...]
            if with_sample:
                os_ref[...] = xs_ref[...] + 0.5 * os_ref[...]

    @pl.when(i == 0)
    def _():
        half_step(True)

    @pl.when(i > 0)
    def _():
        half_step(False)


def _ffn(x, xs, gain, wi, wo, layer, *, tm, tf):
    M, D = x.shape
    Ms = xs.shape[0]
    F = wo.shape[1]
    n_ff = F // tf
    assert M % tm == 0 and F % tf == 0
    return pl.pallas_call(
        functools.partial(_ffn_body, n_ff=n_ff),
        grid=(M // tm, n_ff),
        in_specs=[
            pl.BlockSpec((tm, D), lambda i, j: (i, 0), pipeline_mode=pl.Buffered(1)),
            pl.BlockSpec((Ms, D), lambda i, j: (0, 0)),
            pl.BlockSpec((1, D), lambda i, j: (0, 0)),
            pl.BlockSpec((None, D, tf), lambda i, j: (layer, 0, j)),
            pl.BlockSpec((None, D, tf), lambda i, j: (layer, 0, j + n_ff)),
            pl.BlockSpec((None, tf, D), lambda i, j: (layer, j, 0)),
        ],
        out_specs=[pl.BlockSpec((tm, D), lambda i, j: (i, 0)), pl.BlockSpec((Ms, D), lambda i, j: (0, 0))],
        out_shape=[jax.ShapeDtypeStruct((M, D), F32), jax.ShapeDtypeStruct((Ms, D), F32)],
        scratch_shapes=[pltpu.VMEM((tm + Ms, D), BF16)],
        compiler_params=_params("arbitrary", "arbitrary"),
        name="ffn",
    )(x, xs, _row(gain), wi, wi, wo)


def _mm_body(*refs, has_gain, act, has_res, n_headnorm):
    it = iter(refs)
    x_ref = next(it)
    g_ref = next(it) if has_gain else None
    w_ref = next(it)
    hg_ref = next(it) if n_headnorm else None
    r_ref = next(it) if has_res else None
    o_ref = next(it)
    h_ref = next(it)
    j = pl.program_id(1)

    @pl.when(j == 0)
    def _():
        x = x_ref[...]
        if has_gain:
            x = _rms(x, g_ref[...])
        h_ref[...] = x.astype(BF16)

    y = jnp.dot(h_ref[...], w_ref[...].astype(BF16), preferred_element_type=F32)
    if act == "gelu":
        y = jax.nn.gelu(y, approximate=True)
    if has_res:
        y = r_ref[...] + y
    o_ref[...] = y

    if n_headnorm:
        @pl.when(j < n_headnorm)
        def _():
            tn = o_ref.shape[1]
            for s in range(tn // LANES):
                sl = slice(s * LANES, (s + 1) * LANES)
                o_ref[:, sl] = _rms(y[:, sl], hg_ref[:, sl])


def _mm(x, w, layer, *, tm, tn, n_cols=None, gain=None, act=None, res=None, headnorm=None):
    M, K = x.shape
    N = n_cols if n_cols is not None else w.shape[2]
    assert M % tm == 0 and N % tn == 0
    args, specs = [x], [pl.BlockSpec((tm, K), lambda i, j: (i, 0))]
    if gain is not None:
        args.append(_row(gain))
        specs.append(pl.BlockSpec((1, K), lambda i, j: (0, 0)))
    args.append(w)
    specs.append(pl.BlockSpec((None, K, tn), lambda i, j: (layer, 0, j)))
    n_headnorm = 0
    if headnorm is not None:
        hg, n_headnorm = headnorm
        args.append(hg)
        specs.append(pl.BlockSpec((1, tn), lambda i, j: (0, j)))
    if res is not None:
        args.append(res)
        specs.append(pl.BlockSpec((tm, tn), lambda i, j: (i, j)))
    return pl.pallas_call(
        functools.partial(_mm_body, has_gain=gain is not None, act=act, has_res=res is not None,
                          n_headnorm=n_headnorm),
        grid=(M // tm, N // tn),
        in_specs=specs,
        out_specs=pl.BlockSpec((tm, tn), lambda i, j: (i, j)),
        out_shape=jax.ShapeDtypeStruct((M, N), F32),
        scratch_shapes=[pltpu.VMEM((tm, K), BF16)],
        compiler_params=_params("parallel", "arbitrary"),
        name="mm",
    )(*args)


def _memattn_body(x_ref, gx_ref, wq_ref, gq_ref, k_ref, v_ref, wo_ref, o_ref):
    x = x_ref[...]
    h = _rms(x, gx_ref[...])
    q = _bdot(h, wq_ref[...])
    scale = MEM_HD ** -0.5
    outs = []
    for hd in range(MEM_HEADS):
        sl = slice(hd * MEM_HD, (hd + 1) * MEM_HD)
        qh = _rms(q[:, sl], gq_ref[...])
        s = _bdot_nt(qh, k_ref[:, sl]) * scale
        s = s - jnp.max(s, axis=-1, keepdims=True)
        e = jnp.exp(s)
        p = e / jnp.sum(e, axis=-1, keepdims=True)
        outs.append(_bdot(p, v_ref[:, sl]))
    o = jnp.concatenate(outs, axis=-1)
    o_ref[...] = x + _bdot(o, wo_ref[...])


def _mem_attend(x, gx, wq, gq, k, v, wo, layer, *, rows_per_seq, tm):
    M, D = x.shape
    B, N, W = k.shape
    nt = rows_per_seq // tm
    assert rows_per_seq % tm == 0 and M == B * rows_per_seq
    return pl.pallas_call(
        _memattn_body,
        grid=(B, nt),
        in_specs=[
            pl.BlockSpec((tm, D), lambda b, i: (b * nt + i, 0)),
            pl.BlockSpec((1, D), lambda b, i: (0, 0)),
            pl.BlockSpec((None, D, W), lambda b, i: (layer, 0, 0)),
            pl.BlockSpec((1, MEM_HD), lambda b, i: (0, 0)),
            pl.BlockSpec((None, N, W), lambda b, i: (b, 0, 0)),
            pl.BlockSpec((None, N, W), lambda b, i: (b, 0, 0)),
            pl.BlockSpec((None, W, D), lambda b, i: (layer, 0, 0)),
        ],
        out_specs=pl.BlockSpec((tm, D), lambda b, i: (b * nt + i, 0)),
        out_shape=jax.ShapeDtypeStruct((M, D), F32),
        compiler_params=_params("parallel", "arbitrary"),
        name="mem_attend",
    )(x, _row(gx), wq, _row(gq), k, v, wo)


def _gla_body(q_ref, k_ref, v_ref, r_ref, a_ref, wg_ref, bg_ref, go_ref, s0_ref, y_ref, s_ref,
              *, c, valid, dk, dv):
    i = pl.program_id(1)

    @pl.when(i == 0)
    def _():
        s_ref[...] = s0_ref[...]

    R = LANES
    rows = lax.broadcasted_iota(jnp.int32, (R, 1), 0)
    live = rows < valid
    tri_r = (lax.broadcasted_iota(jnp.int32, (R, R), 1) <= lax.broadcasted_iota(jnp.int32, (R, R), 0))
    tri_b = tri_r.astype(BF16)
    tri_c = tri_r[:c, :]

    def pad(x):
        return jnp.concatenate([x, jnp.zeros((R - c, x.shape[1]), F32)], axis=0)

    heads = range(GLA_HEADS)
    kss = [slice(hd * dk, (hd + 1) * dk) for hd in heads]
    vss = [slice(hd * dv, (hd + 1) * dv) for hd in heads]
    for u in range(q_ref.shape[0] // c):
        rs = slice(u * c, (u + 1) * c)
        a = pad(a_ref[rs, :])
        las = [_bdot(a, wg_ref[:, ks]) + bg_ref[:, ks] for ks in kss]
        parts = []
        for la in las:
            la = (jnp.minimum(la, 0.0) - _softplus_core(la)) / GLA_TAU
            parts.append(jnp.concatenate(_split3(jnp.where(live, la, 0.0)), axis=1))
        bs = []
        for p in parts:
            c3 = jnp.dot(tri_b, p, preferred_element_type=F32)
            bs.append(c3[:, :dk] + c3[:, dk:2 * dk] + c3[:, 2 * dk:])
        vs_, qds, kds, sts = [], [], [], []
        for hd in heads:
            b = bs[hd]
            b_last = b[R - 1:R, :]
            k = jnp.where(live, pad(k_ref[rs, kss[hd]]), 0.0)
            vs_.append(jnp.where(live, pad(v_ref[rs, vss[hd]]), 0.0))
            qds.append(q_ref[rs, kss[hd]] * (dk ** -0.5) * jnp.exp(b[:c, :]))
            kds.append(k * jnp.exp(-b))
            sts.append(jnp.where(rows < R - 1, k * jnp.exp(b_last - b), jnp.exp(b_last)))
        atts = [jnp.where(tri_c, _bdot_nt(qds[hd], kds[hd]), 0.0) for hd in heads]
        states = [s_ref[0, hd] for hd in heads]
        os_ = [_bdot(atts[hd], vs_[hd]) + _bdot(qds[hd], states[hd]) for hd in heads]
        for hd in heads:
            st = sts[hd].T
            s_ref[0, hd] = st[:, R - 1:R] * states[hd] + _bdot(st, vs_[hd])
        for hd in heads:
            r = r_ref[rs, vss[hd]]
            y_ref[rs, vss[hd]] = _rms(os_[hd], go_ref[...]) * (r * jax.nn.sigmoid(r))


def _gla(z, a, w_gate_pad, b_gate, g_out, s0, *, rows_per_seq, c, valid):
    M = z.shape[0]
    B, H, dk, dv = s0.shape
    hk, hv = H * dk, H * dv
    rb = c
    n = rows_per_seq // rb
    assert rows_per_seq % rb == 0 and c < LANES and hv == 2 * hk
    body = functools.partial(_gla_body, c=c, valid=valid, dk=dk, dv=dv)
    return pl.pallas_call(
        body,
        grid=(B, n),
        in_specs=[
            pl.BlockSpec((rb, hk), lambda b, i: (b * n + i, 0)),
            pl.BlockSpec((rb, hk), lambda b, i: (b * n + i, 1)),
            pl.BlockSpec((rb, hv), lambda b, i: (b * n + i, 1)),
            pl.BlockSpec((rb, hv), lambda b, i: (b * n + i, 2)),
            pl.BlockSpec((rb, LANES), lambda b, i: (b * n + i, 0)),
            pl.BlockSpec((LANES, hk), lambda b, i: (0, 0)),
            pl.BlockSpec((1, hk), lambda b, i: (0, 0)),
            pl.BlockSpec((1, dv), lambda b, i: (0, 0)),
            pl.BlockSpec((1, H, dk, dv), lambda b, i: (b, 0, 0, 0)),
        ],
        out_specs=[
            pl.BlockSpec((rb, hv), lambda b, i: (b * n + i, 0)),
            pl.BlockSpec((1, H, dk, dv), lambda b, i: (b, 0, 0, 0)),
        ],
        out_shape=[jax.ShapeDtypeStruct((M, hv), F32), jax.ShapeDtypeStruct((B, H, dk, dv), F32)],
        compiler_params=_params("parallel", "arbitrary"),
        name="gla",
    )(z, z, z, z, a, w_gate_pad, b_gate.reshape(1, hk), g_out.reshape(1, dv), s0)


def _sg_body(u_ref, v_ref, gv_ref, ws_ref, bs_ref, *out_refs, c, n_chunks, gw):
    y_ref = out_refs[-1]
    R = ws_ref.shape[1]
    tri = (lax.broadcasted_iota(jnp.int32, (R, R), 1) <= lax.broadcasted_iota(jnp.int32, (R, R), 0))
    vn = _rms(v_ref[...], gv_ref[...])
    if len(out_refs) == 2:
        out_refs[0][...] = vn
    for g in range(SG_GROUPS):
        w = jnp.where(tri, ws_ref[g], 0.0).astype(BF16)
        gs = slice(g * gw, (g + 1) * gw)
        for t in range(n_chunks):
            rs = slice(t * c, (t + 1) * c)
            vc = vn[rs, gs]
            if c < R:
                vc = jnp.concatenate([vc, jnp.zeros((R - c, gw), F32)], axis=0)
            mixed = jnp.dot(w, vc.astype(BF16), preferred_element_type=F32)[:c, :] + bs_ref[:, gs]
            y_ref[rs, gs] = u_ref[rs, gs] * mixed


def _sg(uv, g_v, w_s, b_full, *, c, n_chunks, emit_vn):
    M, W2 = uv.shape
    W = W2 // 2
    tm = c * n_chunks
    assert M % tm == 0
    body = functools.partial(_sg_body, c=c, n_chunks=n_chunks, gw=W // SG_GROUPS)
    n_out = 2 if emit_vn else 1
    return pl.pallas_call(
        body,
        grid=(M // tm,),
        in_specs=[
            pl.BlockSpec((tm, W), lambda i: (i, 0)),
            pl.BlockSpec((tm, W), lambda i: (i, 1)),
            pl.BlockSpec((1, W), lambda i: (0, 0)),
            pl.BlockSpec(w_s.shape, lambda i: (0, 0, 0)),
            pl.BlockSpec((c, W), lambda i: (0, 0)),
        ],
        out_specs=[pl.BlockSpec((tm, W), lambda i: (i, 0))] * n_out,
        out_shape=[jax.ShapeDtypeStruct((M, W), F32)] * n_out,
        compiler_params=_params("parallel"),
        name="sg",
    )(uv, uv, g_v.reshape(1, W), w_s, b_full)


def _rmsnorm_body(x_ref, g_ref, o_ref):
    o_ref[...] = _rms(x_ref[...], g_ref[...])


def _rmsnorm(x, gain):
    M, D = x.shape
    return pl.pallas_call(
        _rmsnorm_body,
        grid=(1,),
        in_specs=[pl.BlockSpec((M, D), lambda i: (0, 0)), pl.BlockSpec((1, D), lambda i: (0, 0))],
        out_specs=pl.BlockSpec((M, D), lambda i: (0, 0)),
        out_shape=jax.ShapeDtypeStruct((M, D), F32),
        name="rmsnorm",
    )(x, gain.reshape(1, D))


def _pool_body(*refs, tm, gw, normalize, has_res):
    if has_res:
        x_ref, halo_ref, res_ref, g_ref, wp_ref, sc_ref, h_ref, o_ref = refs
    else:
        x_ref, halo_ref, g_ref, wp_ref, sc_ref, h_ref, o_ref = refs
        res_ref = x_ref
    i = pl.program_id(1)
    h, halo = x_ref[...], halo_ref[...]
    if normalize:
        h, halo = _rms(h, g_ref[...]), _rms(halo, g_ref[...])
    h_ref[...] = h
    halo = jnp.where(i > 0, halo, 0.0)
    hc = jnp.concatenate([halo, h], axis=0)
    pos = i * tm + lax.broadcasted_iota(jnp.int32, (tm, 1), 0)
    for g, w in enumerate(POOL_WINDOWS):
        gs = slice(g * gw, (g + 1) * gw)
        s = hc[:, gs]
        span, n = 1, s.shape[0]
        while span < w:
            s = s[span:, :] + s[:n - span, :]
            n -= span
            span *= 2
        win = s[n - tm:, :]
        cnt = jnp.minimum(pos + 1, w).astype(F32)
        pooled = win / cnt - h[:, gs]
        o_ref[:, gs] = res_ref[:, gs] + _bdot(pooled, wp_ref[g]) * sc_ref[:, gs]


def _pool(x, res, gain, w_pool, scale, layer, *, rows_per_seq, tm, normalize):
    M, D = x.shape
    B = M // rows_per_seq
    nt = rows_per_seq // tm
    hb = tm // POOL_HALO
    G = len(POOL_WINDOWS)
    gw = D // G
    assert rows_per_seq % tm == 0 and tm % POOL_HALO == 0
    body = functools.partial(_pool_body, tm=tm, gw=gw, normalize=normalize, has_res=res is not None)
    row_spec = pl.BlockSpec((tm, D), lambda b, i: (b * nt + i, 0))
    args = [x, x] + ([res] if res is not None else []) + [gain.reshape(1, D), w_pool, scale.reshape(1, D)]
    specs = ([row_spec, pl.BlockSpec((POOL_HALO, D), lambda b, i: (jnp.maximum((b * nt + i) * hb - 1, 0), 0))]
             + ([row_spec] if res is not None else [])
             + [pl.BlockSpec((1, D), lambda b, i: (0, 0)),
                pl.BlockSpec((None, G, gw, gw), lambda b, i: (layer, 0, 0, 0)),
                pl.BlockSpec((1, D), lambda b, i: (0, 0))])
    return pl.pallas_call(
        body,
        grid=(B, nt),
        in_specs=specs,
        out_specs=[pl.BlockSpec((tm, D), lambda b, i: (b, 0)), row_spec],
        out_shape=[jax.ShapeDtypeStruct((B * tm, D), F32), jax.ShapeDtypeStruct((M, D), F32)],
        compiler_params=_params("arbitrary", "arbitrary"),
        name="pool",
    )(*args)


def _sb_body(bias_ref, q_ref, k_ref, v_ref, o_ref, carry_ref, *, t, nh, scale):
    hp = pl.program_id(1)
    qi = pl.program_id(2)
    later_m = _later_matrix(t)
    causal = (lax.broadcasted_iota(jnp.int32, (t, t), 1) < lax.broadcasted_iota(jnp.int32, (t, t), 0))

    heads = [slice(h * LANES, (h + 1) * LANES) for h in range(nh)]

    def tiles(starts, diagonal):
        chains = [(s, h) for s in range(len(starts)) for h in range(nh)]
        zs = [_bdot_nt(q_ref[:, heads[h]], k_ref[pl.ds(starts[s], t), heads[h]]) for s, h in chains]
        log_betas, log_1ms = [], []
        for (s, h), z in zip(chains, zs):
            z2 = z * (scale * LOG2E) + bias_ref[hp * nh + h] * LOG2E
            core = jnp.log2(1.0 + jnp.exp2(-jnp.abs(z2)))
            log_beta = jnp.minimum(z2, 0.0) - core
            log_1m = log_beta - z2
            if diagonal and s == 0:
                log_1m = jnp.where(causal, log_1m, 0.0)
            log_betas.append(log_beta)
            log_1ms.append(log_1m)
        laters = [jnp.dot(l.astype(BF16), later_m, preferred_element_type=F32) for l in log_1ms]
        ws = []
        for s, h in chains:
            n = s * nh + h
            first = diagonal and s == 0
            between = laters[n] if first else laters[n] + carry_ref[h]
            carry_ref[h] = between[:, 0:1] + log_1ms[n][:, 0:1]
            w = jnp.exp2(log_betas[n] + between)
            ws.append(jnp.where(causal, w, 0.0) if first else w)
        for h in range(nh):
            pv = _bdot(ws[h], v_ref[pl.ds(starts[0], t), heads[h]])
            for s in range(1, len(starts)):
                pv = pv + _bdot(ws[s * nh + h], v_ref[pl.ds(starts[s], t), heads[h]])
            if diagonal:
                o_ref[:, heads[h]] = pv
            else:
                o_ref[:, heads[h]] += pv

    def at(tile_index):
        return pl.multiple_of(tile_index * t, t)

    odd = qi % 2

    @pl.when(odd == 0)
    def _():
        tiles([at(qi)], True)

    @pl.when(odd == 1)
    def _():
        tiles([at(qi), at(qi - 1)], True)

    def step(it, _):
        newest = qi - odd - 1 - 2 * it
        tiles([at(newest), at(newest - 1)], False)
        return 0

    lax.fori_loop(0, (qi - odd) // 2, step, 0)


def _sb_prompt(qkv, bias, *, rows_per_seq, t):
    M, D3 = qkv.shape
    D = D3 // 3
    hd = D // SB_HEADS
    nh = SB_HEADS_PER_STEP
    B = M // rows_per_seq
    nq = rows_per_seq // t
    groups = SB_HEADS // nh
    assert hd == LANES and rows_per_seq % t == 0 and SB_HEADS % nh == 0
    body = functools.partial(_sb_body, t=t, nh=nh, scale=hd ** -0.5)
    return pl.pallas_call(
        body,
        grid=(B, groups, nq),
        in_specs=[
            pl.BlockSpec(memory_space=pltpu.SMEM),
            pl.BlockSpec((t, nh * hd), lambda b, g, i: (b * nq + i, g)),
            pl.BlockSpec((rows_per_seq, nh * hd), lambda b, g, i: (b, groups + g)),
            pl.BlockSpec((rows_per_seq, nh * hd), lambda b, g, i: (b, 2 * groups + g)),
        ],
        out_specs=pl.BlockSpec((t, nh * hd), lambda b, g, i: (b * nq + i, g)),
        out_shape=jax.ShapeDtypeStruct((M, D), F32),
        scratch_shapes=[pltpu.VMEM((nh, t, 1), F32)],
        compiler_params=_params("parallel", "parallel", "arbitrary"),
        name="sb_prompt",
    )(bias, qkv, qkv, qkv)


def _sbs_body(pt_ref, q_ref, bias_ref, pick_ref, spread_ref, *refs, scale, n_steps, G):
    k_refs, v_refs = refs[:G], refs[G:2 * G]
    o_ref, acc_ref, carry_ref = refs[2 * G:]
    j = pl.program_id(1)
    H, hd = q_ref.shape
    ps = k_refs[0].shape[0]
    flat = ps * H

    @pl.when(j == 0)
    def _():
        acc_ref[...] = jnp.zeros_like(acc_ref)
        carry_ref[...] = jnp.zeros_like(carry_ref)

    own = (lax.broadcasted_iota(jnp.int32, (H, flat), 1) % H) == lax.broadcasted_iota(jnp.int32, (H, flat), 0)
    q = q_ref[...]
    nt = (((1,), (1,)), ((), ()))
    raw = [jnp.where(own, lax.dot_general(q, k_refs[g][...].reshape(flat, hd), nt,
                                          preferred_element_type=F32), 0.0) for g in range(G)]
    raw = jnp.concatenate(raw, axis=0)
    hi, mid, lo = _split3(raw)
    pick = pick_ref[...]
    z = (jnp.dot(hi, pick, preferred_element_type=F32) + jnp.dot(mid, pick, preferred_element_type=F32)
         + jnp.dot(lo, pick, preferred_element_type=F32))
    z = z * scale + bias_ref[...]
    core = _softplus_core(z)
    log_beta = jnp.minimum(z, 0.0) - core
    log_1m = jnp.minimum(-z, 0.0) - core
    l_hi, l_lo = _split2(log_1m)
    later_m = _later_matrix(ps)
    later = jnp.dot(l_hi, later_m, preferred_element_type=F32) + jnp.dot(l_lo, later_m, preferred_element_type=F32)
    total = later[:, 0:1] + log_1m[:, 0:1]
    carry = carry_ref[...]
    betweens = []
    for g in range(G):
        rs = slice(g * H, (g + 1) * H)
        betweens.append(later[rs, :] + carry)
        carry = carry + total[rs, :]
    carry_ref[...] = carry
    w = jnp.exp(log_beta + jnp.concatenate(betweens, axis=0)).astype(BF16)
    wide = jnp.dot(w, spread_ref[...], preferred_element_type=F32)
    acc = acc_ref[...]
    for g in range(G):
        wg = jnp.where(own, wide[g * H:(g + 1) * H, :], 0.0)
        acc = acc + jnp.dot(wg, v_refs[g][...].reshape(flat, hd), preferred_element_type=F32)
    acc_ref[...] = acc

    @pl.when(j == n_steps - 1)
    def _():
        o_ref[...] = acc


def _sb_sample(q, bias, cache_k, cache_v, page_table, layer):
    B, H, hd = q.shape
    n_pages = page_table.shape[1]
    ps = cache_k.shape[2]
    G = SB_PAGES_PER_STEP
    assert n_pages % G == 0
    n_steps = n_pages // G
    flat = ps * H
    pick = (jnp.arange(flat)[:, None] // H == jnp.arange(ps)[None, :]).astype(BF16)
    spread = pick.T
    bias_t = jnp.broadcast_to(jnp.tile(bias.astype(F32), G)[:, None], (G * H, ps))

    def page_spec(g):
        return pl.BlockSpec((None, None, ps, H, hd),
                            lambda b, j, pt: (layer, pt[b, n_pages - 1 - (j * G + g)], 0, 0, 0))

    body = functools.partial(_sbs_body, scale=hd ** -0.5, n_steps=n_steps, G=G)
    grid_spec = pltpu.PrefetchScalarGridSpec(
        num_scalar_prefetch=1,
        grid=(B, n_steps),
        in_specs=[
            pl.BlockSpec((None, H, hd), lambda b, j, pt: (b, 0, 0)),
            pl.BlockSpec((G * H, ps), lambda b, j, pt: (0, 0)),
            pl.BlockSpec((flat, ps), lambda b, j, pt: (0, 0)),
            pl.BlockSpec((ps, flat), lambda b, j, pt: (0, 0)),
        ] + [page_spec(g) for g in range(G)] + [page_spec(g) for g in range(G)],
        out_specs=pl.BlockSpec((None, H, hd), lambda b, j, pt: (b, 0, 0)),
        scratch_shapes=[pltpu.VMEM((H, hd), F32), pltpu.VMEM((H, 1), F32)],
    )
    return pl.pallas_call(
        body,
        grid_spec=grid_spec,
        out_shape=jax.ShapeDtypeStruct((B, H, hd), F32),
        compiler_params=_params("parallel", "arbitrary"),
        name="sb_sample",
    )(page_table, q, bias_t, pick, spread, *([cache_k] * G), *([cache_v] * G))


def _row_tile(m, want):
    t = min(m, want)
    while m % t:
        t //= 2
    return t


def kernel(x_prompt, x_sample, cache_mem_k, cache_mem_v, state_gla, state_pool, cache_sb_k, cache_sb_v, page_table, mem_prompt, ffn_a_norm, ffn_a_wi, ffn_a_wo, mix_norm, gla_w_in, gla_w_gate, gla_b_gate, gla_out_norm, gla_w_out, sg_w_in, sg_v_norm, sg_w_s, sg_b_s, sg_w_out, pool_w, pool_scale, sb_w_qkv, sb_q_norm, sb_k_norm, sb_logit_bias, sb_w_out, mem_x_norm, mem_in_norm, mem_w_q, mem_q_norm, mem_w_k, mem_k_norm, mem_w_v, mem_w_o, ffn_b_norm, ffn_b_wi, ffn_b_wo):
    Bp, T, D = x_prompt.shape
    Bs = x_sample.shape[0]
    depth = ffn_a_norm.shape[0]
    n_mem = mem_prompt.shape[1]
    mem_w = MEM_HEADS * MEM_HD
    P = SAMPLE_PAD
    Mp, Ms = Bp * T, Bs * P

    xp = x_prompt.reshape(Mp, D)
    xs = jnp.pad(x_sample, ((0, 0), (0, P - 1), (0, 0))).reshape(Ms, D)

    tmp = _row_tile(Mp, 1024)
    tf = 512
    tn = 1024

    def first_rows(a):
        return a.reshape(Bs, P, -1)[:, 0]

    def pad_rows(a):
        return jnp.pad(a[:, None, :], ((0, 0), (0, P - 1), (0, 0))).reshape(Bs * P, -1)

    mem_flat = mem_prompt.reshape(Bp * n_mem, D)
    mem_wq_b, mem_wo_b = mem_w_q.astype(BF16), mem_w_o.astype(BF16)
    gla_w_in_b, gla_w_out_b = gla_w_in.astype(BF16), gla_w_out.astype(BF16)
    sg_w_in_b, sg_w_out_b = sg_w_in.astype(BF16), sg_w_out.astype(BF16)
    sb_w_qkv_b, sb_w_out_b = sb_w_qkv.astype(BF16), sb_w_out.astype(BF16)

    def out_proj(y, w_b, layer, res):
        return _mm(y, w_b, layer, tm=_row_tile(y.shape[0], 512), tn=w_b.shape[2], res=res)
    mem_k_new, mem_v_new = [], []
    gla_p, gla_s, pool_p, pool_s, sg_s = [], [], [], [], []
    sbk_p, sbv_p, sbk_s, sbv_s = [], [], [], []

    for i in range(depth):
        kind, j = i % 4, i // 4
        xp, xs = _ffn(xp, xs, ffn_a_norm[i], ffn_a_wi, ffn_a_wo, i, tm=tmp, tf=tf)

        if kind == 0:
            H, dk = GLA_HEADS, gla_w_gate.shape[2] // GLA_HEADS
            hk = H * dk
            hv = gla_w_out.shape[1]
            n_main = 2 * hk + 2 * hv
            w_a = jnp.pad(gla_w_in_b[j:j + 1, :, n_main:], ((0, 0), (0, 0), (0, LANES - GLA_RANK)))
            w_g = jnp.pad(gla_w_gate[j], ((0, LANES - GLA_RANK), (0, 0)))
            s0p = jnp.zeros((Bp, H, dk, hv // H), F32)
            outs = []
            for x, rows, c, valid, s0, tm in ((xp, T, GLA_CHUNK, GLA_CHUNK, s0p, tmp),
                                              (xs, P, P, 1, state_gla[j], Ms)):
                z = _mm(x, gla_w_in_b, j, tm=tm, tn=tn, n_cols=n_main, gain=mix_norm[i])
                a = _mm(x, w_a, 0, tm=tm, tn=LANES, gain=mix_norm[i])
                y, st = _gla(z, a, w_g, gla_b_gate[j], gla_out_norm[j], s0, rows_per_seq=rows, c=c, valid=valid)
                outs.append((out_proj(y, gla_w_out_b, j, x), st))
            (xp, st_p), (xs, st_s) = outs
            gla_p.append(st_p)
            gla_s.append(st_s)
        elif kind == 1:
            W = sg_w_out.shape[1]
            outs = []
            for x, c, nck, tm, emit_vn in ((xp, SG_CHUNK, 4, tmp, False), (xs, P, 1, Ms, True)):
                uv = _mm(x, sg_w_in_b, j, tm=tm, tn=tn, gain=mix_norm[i], act="gelu")
                b_full = jnp.repeat(sg_b_s[j][:, :c].T, W // SG_GROUPS, axis=1)
                w_s = jnp.pad(sg_w_s[j][:, :c, :c], ((0, 0), (0, SG_CHUNK - c), (0, SG_CHUNK - c)))
                res = _sg(uv, sg_v_norm[j], w_s, b_full, c=c, n_chunks=nck, emit_vn=emit_vn)
                outs.append((out_proj(res[-1], sg_w_out_b, j, x), res[0]))
            (xp, _), (xs, vn_s) = outs
            sg_s.append(first_rows(vn_s)[:, None, :])
        elif kind == 2:
            tmq = _row_tile(T, 512)
            hp, xp = _pool(xp, None, mix_norm[i], pool_w, pool_scale[j], j, rows_per_seq=T, tm=tmq, normalize=True)
            pool_p.append(hp.reshape(Bp, tmq, D)[:, tmq - POOL_BUF:])
            assert POOL_BUF + 1 == POOL_HALO
            hs = first_rows(_rmsnorm(xs, mix_norm[i]))
            hc = jnp.concatenate([state_pool[j], hs[:, None, :]], axis=1)
            res = jnp.pad(first_rows(xs)[:, None, :], ((0, 0), (POOL_BUF, 0), (0, 0)))
            _, out = _pool(hc.reshape(Bs * POOL_HALO, D), res.reshape(Bs * POOL_HALO, D), mix_norm[i],
                           pool_w, pool_scale[j], j, rows_per_seq=POOL_HALO, tm=POOL_HALO, normalize=False)
            xs = pad_rows(out.reshape(Bs, POOL_HALO, D)[:, POOL_HALO - 1])
            pool_s.append(hc[:, POOL_HALO - POOL_BUF:])
        else:
            hd = D // SB_HEADS
            hg = jnp.concatenate([jnp.tile(sb_q_norm[j], SB_HEADS), jnp.tile(sb_k_norm[j], SB_HEADS),
                                  jnp.ones((D,), F32)]).reshape(1, 3 * D)
            n_norm = 2 * D // tn
            qkv_p = _mm(xp, sb_w_qkv_b, j, tm=tmp, tn=tn, gain=mix_norm[i], headnorm=(hg, n_norm))
            att_p = _sb_prompt(qkv_p, sb_logit_bias[j], rows_per_seq=T, t=256)
            xp = out_proj(att_p, sb_w_out_b, j, xp)
            sbk_p.append(qkv_p[:, D:2 * D].reshape(Bp, T, SB_HEADS, hd))
            sbv_p.append(qkv_p[:, 2 * D:].reshape(Bp, T, SB_HEADS, hd))
            qkv_s = first_rows(_mm(xs, sb_w_qkv_b, j, tm=Ms, tn=tn, gain=mix_norm[i], headnorm=(hg, n_norm)))
            q_s, k_s, v_s = (qkv_s[:, n * D:(n + 1) * D].reshape(Bs, SB_HEADS, hd) for n in range(3))
            att_s = _sb_sample(q_s, sb_logit_bias[j], cache_sb_k, cache_sb_v, page_table, j)
            xs = out_proj(pad_rows(att_s.reshape(Bs, D)), sb_w_out_b, j, xs)
            sbk_s.append(k_s[:, None])
            sbv_s.append(v_s[:, None])

        hg_k = jnp.tile(mem_k_norm[i], MEM_HEADS).reshape(1, mem_w)
        mk = _mm(mem_flat, mem_w_k, i, tm=Bp * n_mem, tn=mem_w, gain=mem_in_norm[i], headnorm=(hg_k, 1))
        mv = _mm(mem_flat, mem_w_v, i, tm=Bp * n_mem, tn=mem_w, gain=mem_in_norm[i])
        mem_k_new.append(mk.reshape(Bp, n_mem, MEM_HEADS, MEM_HD))
        mem_v_new.append(mv.reshape(Bp, n_mem, MEM_HEADS, MEM_HD))
        xp = _mem_attend(xp, mem_x_norm[i], mem_wq_b, mem_q_norm[i], mk.reshape(Bp, n_mem, mem_w),
                         mv.reshape(Bp, n_mem, mem_w), mem_wo_b, i, rows_per_seq=T, tm=_row_tile(T, 512))
        xs = _mem_attend(xs, mem_x_norm[i], mem_wq_b, mem_q_norm[i], cache_mem_k[i].reshape(Bs, n_mem, mem_w),
                         cache_mem_v[i].reshape(Bs, n_mem, mem_w), mem_wo_b, i, rows_per_seq=P, tm=P)

        xp, xs = _ffn(xp, xs, ffn_b_norm[i], ffn_b_wi, ffn_b_wo, i, tm=tmp, tf=tf)

    return (xp.reshape(Bp, T, D), first_rows(xs)[:, None, :], jnp.stack(mem_k_new), jnp.stack(mem_v_new),
            jnp.stack(gla_p), jnp.stack(gla_s), jnp.stack(pool_p), jnp.stack(pool_s), jnp.stack(sg_s),
            jnp.stack(sbk_p), jnp.stack(sbv_p), jnp.stack(sbk_s), jnp.stack(sbv_s))
```
